```python
import jax, jax.numpy as jnp
from jax import lax
import numpy as np

D_MODEL = 1024
BATCH = 8
SEQ = 2048
DEPTH = 4
DEC_BATCH = 128
DEC_SEQ = 1
PAST_LEN = 2048
PAGE_SIZE = 128

N_META = 16
N_A_LAYERS = DEPTH // 2
N_B_LAYERS = DEPTH - N_A_LAYERS
A_HEADS = 4
A_DK_TOT = D_MODEL // 2
A_DV_TOT = D_MODEL
A_DK = A_DK_TOT // A_HEADS
A_DV = A_DV_TOT // A_HEADS
A_GATE_RANK = 16
A_GATE_TAU = 16.0
A_CHUNK = 64
A_IN = 2 * A_DK_TOT + 2 * A_DV_TOT + A_GATE_RANK
B_HEADS = 16
B_HD = D_MODEL // B_HEADS
B_QBLOCK = 128
KV_IN = 2 * D_MODEL + B_HEADS
D_FF = ((8 * D_MODEL // 3 + 255) // 256) * 256
EPS = 1e-6

kernel_name = 'gla_fox_yoco_macaron_meta_step'


def rmsnorm(x, g):
    xf = x.astype(jnp.float32)
    y = xf * lax.rsqrt(jnp.mean(xf * xf, axis=-1, keepdims=True) + EPS)
    return (y * g.astype(jnp.float32)).astype(x.dtype)


def swiglu(x, w_gate, w_up, w_down):
    return (jax.nn.silu(x @ w_gate) * (x @ w_up)) @ w_down


def gla_project(h, w_in, w_alpha2, b_alpha):
    bsz, t = h.shape[0], h.shape[1]
    p = h @ w_in
    q, k, v, r, a = jnp.split(p, [A_DK_TOT, 2 * A_DK_TOT, 2 * A_DK_TOT + A_DV_TOT, 2 * A_DK_TOT + 2 * A_DV_TOT], axis=-1)
    la = jax.nn.log_sigmoid((a @ w_alpha2 + b_alpha).astype(jnp.float32)) / A_GATE_TAU
    q = q.reshape(bsz, t, A_HEADS, A_DK) * (A_DK ** -0.5)
    k = k.reshape(bsz, t, A_HEADS, A_DK)
    v = v.reshape(bsz, t, A_HEADS, A_DV)
    la = la.reshape(bsz, t, A_HEADS, A_DK)
    return q, k, v, la, r


def gla_chunk(s0, q, k, v, la):
    qf, kf, vf = q.astype(jnp.float32), k.astype(jnp.float32), v.astype(jnp.float32)
    c = q.shape[1]
    b = jnp.cumsum(la, axis=1)
    causal = jnp.tril(jnp.ones((c, c), dtype=bool))[None, :, :, None, None]
    decay = jnp.exp(jnp.where(causal, b[:, :, None] - b[:, None, :], -jnp.inf))
    scores = jnp.einsum('bthd,bshd,btshd->bhts', qf, kf, decay)
    o = jnp.einsum('bhts,bshv->bthv', scores, vf) + jnp.einsum('bthd,bhdv->bthv', qf * jnp.exp(b), s0)
    b_end = b[:, -1]
    s1 = jnp.exp(b_end)[..., None] * s0 + jnp.einsum('bshd,bshv->bhdv', kf * jnp.exp(b_end[:, None] - b), vf)
    return s1, o


def gla_sequence(s0, q, k, v, la):
    bsz, t = q.shape[0], q.shape[1]
    n = t // A_CHUNK

    def chunks(a):
        return jnp.moveaxis(a.reshape((bsz, n, A_CHUNK) + a.shape[2:]), 1, 0)

    s, o = lax.scan(lambda st, xs: gla_chunk(st, *xs), s0, (chunks(q), chunks(k), chunks(v), chunks(la)))
    return s, jnp.moveaxis(o, 0, 1).reshape(bsz, t, A_HEADS, A_DV)


def gla_output(o, r, g_head, w_out):
    bsz, t = o.shape[0], o.shape[1]
    o = rmsnorm(o, g_head).reshape(bsz, t, A_DV_TOT).astype(r.dtype)
    return (o * jax.nn.silu(r)) @ w_out


def shared_kv(x, g, w_in, b_f):
    bsz, t = x.shape[0], x.shape[1]
    p = rmsnorm(x, g) @ w_in
    k = p[..., :D_MODEL].reshape(bsz, t, B_HEADS, B_HD)
    v = p[..., D_MODEL:2 * D_MODEL].reshape(bsz, t, B_HEADS, B_HD)
    logf = jax.nn.log_sigmoid((p[..., 2 * D_MODEL:] + b_f).astype(jnp.float32))
    return k, v, logf


def fox_attend(q, k, v, c_q, c_k, pos_q, pos_k):
    s = jnp.einsum('bthd,bshd->bhts', q, k).astype(jnp.float32) * (B_HD ** -0.5)
    s = s + jnp.swapaxes(c_q, 1, 2)[:, :, :, None] - jnp.swapaxes(c_k, 1, 2)[:, :, None, :]
    s = jnp.where((pos_k[None, :] <= pos_q[:, None])[None, None], s, -jnp.inf)
    p = jax.nn.softmax(s, axis=-1).astype(v.dtype)
    return jnp.einsum('bhts,bshd->bthd', p, v)


def setup_inputs(seed: int = 0) -> dict:
    key = jax.random.key(seed)
    ks = iter(jax.random.split(key, 40))
    n_pages = PAST_LEN // PAGE_SIZE
    n_used = DEC_BATCH * n_pages
    n_phys = n_used + max(1, n_used // 4)

    def nrm(shape, scale=1.0):
        return jax.random.normal(next(ks), shape, jnp.float32) * scale

    def gain(shape):
        return 1.0 + 0.02 * jax.random.normal(next(ks), shape, jnp.float32)

    page_table = jax.random.permutation(next(ks), n_phys)[:n_used].reshape(DEC_BATCH, n_pages).astype(jnp.int32)
    return {
        'x_prompt': nrm((BATCH, SEQ, D_MODEL)),
        'x_sample': nrm((DEC_BATCH, DEC_SEQ, D_MODEL)),
        'state_gla': nrm((N_A_LAYERS, DEC_BATCH, A_HEADS, A_DK, A_DV), 0.5),
        'cache_k': nrm((n_phys, PAGE_SIZE, B_HEADS, B_HD)),
        'cache_v': nrm((n_phys, PAGE_SIZE, B_HEADS, B_HD)),
        'cache_logf': jax.nn.log_sigmoid(nrm((n_phys, PAGE_SIZE, B_HEADS)) + 2.0),
        'page_table': page_table,
        'meta_tokens': nrm((N_META, D_MODEL)),
        'norm_ffn1': gain((DEPTH, D_MODEL)),
        'ffn1_w_gate': nrm((DEPTH, D_MODEL, D_FF), D_MODEL ** -0.5),
        'ffn1_w_up': nrm((DEPTH, D_MODEL, D_FF), D_MODEL ** -0.5),
        'ffn1_w_down': nrm((DEPTH, D_FF, D_MODEL), D_FF ** -0.5),
        'norm_mix': gain((DEPTH, D_MODEL)),
        'norm_ffn2': gain((DEPTH, D_MODEL)),
        'ffn2_w_gate': nrm((DEPTH, D_MODEL, D_FF), D_MODEL ** -0.5),
        'ffn2_w_up': nrm((DEPTH, D_MODEL, D_FF), D_MODEL ** -0.5),
        'ffn2_w_down': nrm((DEPTH, D_FF, D_MODEL), D_FF ** -0.5),
        'gla_w_in': nrm((N_A_LAYERS, D_MODEL, A_IN), D_MODEL ** -0.5),
        'gla_w_alpha2': nrm((N_A_LAYERS, A_GATE_RANK, A_DK_TOT), A_GATE_RANK ** -0.5),
        'gla_b_alpha': nrm((N_A_LAYERS, A_DK_TOT), 0.02),
        'gla_g_head': gain((N_A_LAYERS, A_DV)),
        'gla_w_out': nrm((N_A_LAYERS, A_DV_TOT, D_MODEL), A_DV_TOT ** -0.5),
        'kv_norm': gain((D_MODEL,)),
        'kv_w_in': nrm((D_MODEL, KV_IN), D_MODEL ** -0.5),
        'kv_b_f': nrm((B_HEADS,), 0.02),
        'fox_w_q': nrm((N_B_LAYERS, D_MODEL, D_MODEL), D_MODEL ** -0.5),
        'fox_w_out': nrm((N_B_LAYERS, D_MODEL, D_MODEL), D_MODEL ** -0.5),
        'final_norm': gain((D_MODEL,)),
    }


def reference(x_prompt, x_sample, state_gla, cache_k, cache_v, cache_logf, page_table, meta_tokens,
              norm_ffn1, ffn1_w_gate, ffn1_w_up, ffn1_w_down, norm_mix, norm_ffn2,
              ffn2_w_gate, ffn2_w_up, ffn2_w_down, gla_w_in, gla_w_alpha2, gla_b_alpha, gla_g_head,
              gla_w_out, kv_norm, kv_w_in, kv_b_f, fox_w_q, fox_w_out, final_norm):

    def trunk(x, gla_mix, kv_ctx, fox_mix):
        states = []
        rows = ctx = None
        for l in range(DEPTH):
            x = x + 0.5 * swiglu(rmsnorm(x, norm_ffn1[l]), ffn1_w_gate[l], ffn1_w_up[l], ffn1_w_down[l])
            h = rmsnorm(x, norm_mix[l])
            if l < N_A_LAYERS:
                y, st = gla_mix(l, h)
                states.append(st)
            else:
                y = fox_mix(l - N_A_LAYERS, h, ctx)
            x = x + y
            x = x + 0.5 * swiglu(rmsnorm(x, norm_ffn2[l]), ffn2_w_gate[l], ffn2_w_up[l], ffn2_w_down[l])
            if l == N_A_LAYERS - 1:
                rows, ctx = kv_ctx(x)
        return rmsnorm(x, final_norm), jnp.stack(states, axis=0), rows

    def gla_prompt(l, h):
        q, k, v, la, r = gla_project(h, gla_w_in[l], gla_w_alpha2[l], gla_b_alpha[l])
        s0 = jnp.zeros((h.shape[0], A_HEADS, A_DK, A_DV), jnp.float32)
        s_m, o_m = gla_chunk(s0, q[:, :N_META], k[:, :N_META], v[:, :N_META], la[:, :N_META])
        s, o_r = gla_sequence(s_m, q[:, N_META:], k[:, N_META:], v[:, N_META:], la[:, N_META:])
        return gla_output(jnp.concatenate([o_m, o_r], axis=1), r, gla_g_head[l], gla_w_out[l]), s

    def kv_prompt(x):
        k, v, logf = shared_kv(x, kv_norm, kv_w_in, kv_b_f)
        return (k, v, logf), (k, v, jnp.cumsum(logf, axis=1))

    def fox_prompt(lb, h, ctx):
        k, v, c = ctx
        bsz, t = h.shape[0], h.shape[1]
        q = (h @ fox_w_q[lb]).reshape(bsz, t, B_HEADS, B_HD)
        pos = jnp.arange(t)
        o_m = fox_attend(q[:, :N_META], k[:, :N_META], v[:, :N_META], c[:, :N_META], c[:, :N_META],
                         pos[:N_META], pos[:N_META])
        nb = (t - N_META) // B_QBLOCK

        def blocks(a):
            return jnp.moveaxis(a[:, N_META:].reshape((bsz, nb, B_QBLOCK) + a.shape[2:]), 1, 0)

        pos_b = pos[N_META:].reshape(nb, B_QBLOCK)
        o_b = lax.map(lambda xs: fox_attend(xs[0], k, v, xs[1], c, xs[2], pos), (blocks(q), blocks(c), pos_b))
        o_r = jnp.moveaxis(o_b, 0, 1).reshape(bsz, t - N_META, B_HEADS, B_HD)
        o = jnp.concatenate([o_m, o_r], axis=1).reshape(bsz, t, D_MODEL)
        return o @ fox_w_out[lb]

    def gla_sample(l, h):
        q, k, v, la, r = gla_project(h, gla_w_in[l], gla_w_alpha2[l], gla_b_alpha[l])
        s, o = gla_chunk(state_gla[l].astype(jnp.float32), q, k, v, la)
        return gla_output(o, r, gla_g_head[l], gla_w_out[l]), s.astype(state_gla.dtype)

    def kv_sample(x):
        k, v, logf = shared_kv(x, kv_norm, kv_w_in, kv_b_f)
        bsz = x.shape[0]
        n_pages = page_table.shape[1]

        def gather(cache):
            return cache[page_table].reshape((bsz, n_pages * PAGE_SIZE) + cache.shape[2:])

        k_all = jnp.concatenate([gather(cache_k).astype(k.dtype), k], axis=1)
        v_all = jnp.concatenate([gather(cache_v).astype(v.dtype), v], axis=1)
        lf_all = jnp.concatenate([gather(cache_logf).astype(jnp.float32), logf], axis=1)
        return (k, v, logf), (k_all, v_all, jnp.cumsum(lf_all, axis=1))

    def fox_sample(lb, h, ctx):
        k_all, v_all, c_all = ctx
        bsz, t = h.shape[0], h.shape[1]
        past = k_all.shape[1] - t
        q = (h @ fox_w_q[lb]).reshape(bsz, t, B_HEADS, B_HD)
        pos_k = jnp.arange(past + t)
        o = fox_attend(q, k_all, v_all, c_all[:, past:], c_all, pos_k[past:], pos_k)
        return o.reshape(bsz, t, D_MODEL) @ fox_w_out[lb]

    meta = jnp.broadcast_to(meta_tokens.astype(x_prompt.dtype)[None], (x_prompt.shape[0], N_META, D_MODEL))
    x0 = jnp.concatenate([meta, x_prompt], axis=1)
    y_p, st_p, rows_p = trunk(x0, gla_prompt, kv_prompt, fox_prompt)
    y_s, st_s, rows_s = trunk(x_sample, gla_sample, kv_sample, fox_sample)
    k_p, v_p, lf_p = rows_p
    k_s, v_s, lf_s = rows_s
    return (y_p[:, N_META:], y_s, st_p, k_p, v_p, lf_p, st_s, k_s, v_s, lf_s)
```

```python
import functools

import jax
import jax.numpy as jnp
from jax import lax
from jax.experimental import pallas as pl
from jax.experimental.pallas import tpu as pltpu

F32 = jnp.float32
BF16 = jnp.bfloat16
HIGHEST = lax.Precision.HIGHEST

D_MODEL = 1024
BATCH = 8
SEQ = 2048
DEPTH = 4
DEC_BATCH = 128
PAST_LEN = 2048
PAGE_SIZE = 128
N_PAGES = PAST_LEN // PAGE_SIZE
N_META = 16
N_A_LAYERS = 2
A_HEADS = 4
A_DK_TOT = 512
A_DV_TOT = 1024
A_DK = 128
A_DV = 256
A_GATE_RANK = 16
A_GATE_TAU = 16.0
B_HEADS = 16
B_HD = 64
D_FF = 2816
EPS = 1e-6

R_MAIN = BATCH * SEQ
R_META = BATCH * N_META
R_SAMPLE = DEC_BATCH
R_PROMPT = R_MAIN + R_META
R_ALL = R_PROMPT + R_SAMPLE

V7X_LANES = 128
V7X_VMEM_LIMIT = 56 * 1024 * 1024

ROW_TILE = 640
FF_CHUNK = 1408
GLA_CHUNK = 128
GLA_SUB = 16
FOX_TQ = 256
FOX_TK = 256
PAGES_PER_STEP = 8
GLA_STEP_BB = 8


def _params(n_grid):
    return pltpu.CompilerParams(dimension_semantics=("arbitrary",) * n_grid,
                                vmem_limit_bytes=V7X_VMEM_LIMIT)


def _rmsnorm(x, g):
    return x * lax.rsqrt(jnp.mean(x * x, axis=-1, keepdims=True) + EPS) * g


def _log_sigmoid(z):
    return jnp.minimum(z, 0.0) - jnp.log1p(jnp.exp(-jnp.abs(z)))


def _silu(x):
    return x * jax.nn.sigmoid(x)


def _dot(a, b, precision=None):
    return jnp.dot(a, b, preferred_element_type=F32, precision=precision)


def _dot_nt(a, b):
    return lax.dot_general(a, b, (((1,), (1,)), ((), ())), preferred_element_type=F32)


def _dot_tn(a, b):
    return lax.dot_general(a, b, (((0,), (0,)), ((), ())), preferred_element_type=F32)


def _resident(a):
    nd = a.ndim
    return pl.BlockSpec(a.shape, lambda *_: (0,) * nd, pipeline_mode=pl.Buffered(1))


def _row_call(body, row_ins, res_ins, outs, name, tm=ROW_TILE):
    rows = row_ins[0].shape[0]
    in_specs = [pl.BlockSpec((tm, a.shape[1]), lambda i: (i, 0)) for a in row_ins]
    in_specs += [_resident(a) for a in res_ins]
    out_specs = [pl.BlockSpec((tm, c), lambda i: (i, 0)) for c, _ in outs]
    out_shape = [jax.ShapeDtypeStruct((rows, c), dt) for c, dt in outs]
    return pl.pallas_call(body, grid=(rows // tm,), in_specs=in_specs, out_specs=out_specs,
                          out_shape=out_shape, compiler_params=_params(1), name=name)(*row_ins, *res_ins)


def _ffn_body(x_ref, g_ref, wg_ref, wu_ref, wd_ref, o_ref):
    x = x_ref[...]
    n = _rmsnorm(x, g_ref[...]).astype(BF16)
    acc = jnp.zeros(x.shape, F32)
    for c in range(D_FF // FF_CHUNK):
        lo, hi = c * FF_CHUNK, (c + 1) * FF_CHUNK
        gate = _dot(n, wg_ref[:, lo:hi])
        up = _dot(n, wu_ref[:, lo:hi])
        acc = acc + _dot((_silu(gate) * up).astype(BF16), wd_ref[lo:hi, :])
    o_ref[...] = x + 0.5 * acc


def _ffn(x, g, wg, wu, wd):
    return _row_call(_ffn_body, [x], [g, wg, wu, wd], [(D_MODEL, F32)], "ffn")[0]


def _gla_proj_body(x_ref, g_ref, wq_ref, wk_ref, wv_ref, wr_ref, wa_ref, w2_ref, ba_ref,
                   q_ref, k_ref, v_ref, r_ref, la_ref):
    n = _rmsnorm(x_ref[...], g_ref[...]).astype(BF16)
    q_ref[...] = _dot(n, wq_ref[...]) * (A_DK ** -0.5)
    k_ref[...] = _dot(n, wk_ref[...])
    v_ref[...] = _dot(n, wv_ref[...])
    r_ref[...] = _dot(n, wr_ref[...])
    a = _dot(n, wa_ref[...]).astype(BF16)
    z = _dot(a, w2_ref[...]) + ba_ref[...]
    la_ref[...] = _log_sigmoid(z) * (1.0 / A_GATE_TAU)


def _gla_proj(x, g, w_in, w_alpha2, b_alpha):
    w = w_in.astype(BF16)
    c0, c1, c2, c3 = A_DK_TOT, 2 * A_DK_TOT, 2 * A_DK_TOT + A_DV_TOT, 2 * A_DK_TOT + 2 * A_DV_TOT
    wa = jnp.pad(w[:, c3:], ((0, 0), (0, V7X_LANES - A_GATE_RANK)))
    w2 = jnp.pad(w_alpha2.astype(BF16), ((0, V7X_LANES - A_GATE_RANK), (0, 0)))
    res = [g, w[:, :c0], w[:, c0:c1], w[:, c1:c2], w[:, c2:c3], wa, w2, b_alpha.reshape(1, A_DK_TOT)]
    outs = [(A_DK_TOT, F32), (A_DK_TOT, F32), (A_DV_TOT, F32), (A_DV_TOT, F32), (A_DK_TOT, F32)]
    return _row_call(_gla_proj_body, [x], res, outs, "gla_proj")


def _gla_chunk(q, k, v, la, st, ecat_ref, n_sub):
    c = GLA_SUB * n_sub
    row = lax.broadcasted_iota(jnp.int32, (c, c), 0)
    col = lax.broadcasted_iota(jnp.int32, (c, c), 1)
    causal = row >= col
    b = _dot(causal.astype(F32), la, precision=HIGHEST)
    a_rows = []
    for i in range(n_sub):
        lo = GLA_SUB * i
        b_i, q_i, k_i = b[lo:lo + GLA_SUB], q[lo:lo + GLA_SUB], k[lo:lo + GLA_SUB]
        pieces = []
        for s in range(GLA_SUB):
            decay = jnp.exp(jnp.minimum(b_i - b_i[s:s + 1], 0.0))
            pieces.append((q_i * k_i[s:s + 1] * decay).astype(BF16))
        a_i = _dot(jnp.concatenate(pieces, axis=1), ecat_ref[i])
        if i > 0:
            r_i = b[lo - 1:lo]
            q_s = (q_i * jnp.exp(b_i - r_i)).astype(BF16)
            k_s = (k[:lo] * jnp.exp(r_i - b[:lo])).astype(BF16)
            k_s = jnp.concatenate([k_s, jnp.zeros((c - lo, A_DK), BF16)], axis=0)
            a_i = a_i + _dot_nt(q_s, k_s)
        a_rows.append(a_i)
    a = jnp.concatenate(a_rows, axis=0)[:, :c]
    a = jnp.where(causal, a, 0.0)
    vb = v.astype(BF16)
    o = _dot(a.astype(BF16), vb)
    b_end = b[c - 1:c]
    k_d = (k * jnp.exp(b_end - b)).astype(BF16)
    st_new = _dot_tn(vb, k_d)
    if st is not None:
        o = o + _dot_nt((q * jnp.exp(b)).astype(BF16), st.astype(BF16))
        st_new = st_new + st * jnp.exp(b_end)
    return o, st_new


def _gla_scan_body(q_ref, k_ref, v_ref, la_ref, qt_ref, kt_ref, vt_ref, lat_ref, ecat_ref,
                   o_ref, ot_ref, s_ref, st_ref):
    o_meta, st = _gla_chunk(qt_ref[...], kt_ref[...], vt_ref[...], lat_ref[...], None, ecat_ref, 1)
    ot_ref[...] = o_meta
    st_ref[...] = st

    def step(j, carry):
        rows = pl.ds(pl.multiple_of(j * GLA_CHUNK, GLA_CHUNK), GLA_CHUNK)
        o, st_new = _gla_chunk(q_ref[rows, :], k_ref[rows, :], v_ref[rows, :], la_ref[rows, :],
                               st_ref[...], ecat_ref, GLA_CHUNK // GLA_SUB)
        o_ref[rows, :] = o
        st_ref[...] = st_new
        return carry

    lax.fori_loop(0, SEQ // GLA_CHUNK, step, 0)
    s_ref[0, 0] = st_ref[...].T


def _gla_ecat():
    n_sub = GLA_CHUNK // GLA_SUB
    key = lax.broadcasted_iota(jnp.int32, (n_sub, GLA_SUB * A_DK, GLA_CHUNK), 1) // A_DK
    sub = lax.broadcasted_iota(jnp.int32, (n_sub, GLA_SUB * A_DK, GLA_CHUNK), 0)
    col = lax.broadcasted_iota(jnp.int32, (n_sub, GLA_SUB * A_DK, GLA_CHUNK), 2)
    return (col == GLA_SUB * sub + key).astype(BF16)


def _gla_scan(q, k, v, la):
    meta0 = R_MAIN // N_META
    main_k = pl.BlockSpec((SEQ, A_DK), lambda b, h: (b, h))
    main_v = pl.BlockSpec((SEQ, A_DV), lambda b, h: (b, h))
    meta_k = pl.BlockSpec((N_META, A_DK), lambda b, h: (meta0 + b, h))
    meta_v = pl.BlockSpec((N_META, A_DV), lambda b, h: (meta0 + b, h))
    ecat = _gla_ecat()
    return pl.pallas_call(
        _gla_scan_body, grid=(BATCH, A_HEADS),
        in_specs=[main_k, main_k, main_v, main_k, meta_k, meta_k, meta_v, meta_k, _resident(ecat)],
        out_specs=[pl.BlockSpec((SEQ, A_DV), lambda b, h: (b, h)),
                   pl.BlockSpec((N_META, A_DV), lambda b, h: (b, h)),
                   pl.BlockSpec((1, 1, A_DK, A_DV), lambda b, h: (b, h, 0, 0))],
        out_shape=[jax.ShapeDtypeStruct((R_MAIN, A_DV_TOT), F32),
                   jax.ShapeDtypeStruct((R_META, A_DV_TOT), F32),
                   jax.ShapeDtypeStruct((BATCH, A_HEADS, A_DK, A_DV), F32)],
        scratch_shapes=[pltpu.VMEM((A_DV, A_DK), F32)],
        compiler_params=_params(2), name="gla_scan")(q, k, v, la, q, k, v, la, ecat)


def _gla_step_body(s0_ref, qt_ref, kt_ref, lat_ref, v_ref, s1_ref, o_ref):
    for bi in range(GLA_STEP_BB):
        for h in range(A_HEADS):
            cols = slice(h * A_DV, (h + 1) * A_DV)
            k_col = kt_ref[0, h, :, bi:bi + 1]
            q_col = qt_ref[0, h, :, bi:bi + 1]
            decay = jnp.exp(lat_ref[0, h, :, bi:bi + 1])
            s1 = decay * s0_ref[bi, h] + k_col * v_ref[bi:bi + 1, cols]
            s1_ref[bi, h] = s1
            o_ref[bi:bi + 1, cols] = jnp.sum(q_col * s1, axis=0, keepdims=True)


def _gla_step(s0, q, k, v, la):
    n_steps = DEC_BATCH // GLA_STEP_BB

    def cols(a):
        a = a[R_PROMPT:].reshape(n_steps, GLA_STEP_BB, A_HEADS, A_DK)
        return jnp.transpose(a, (0, 2, 3, 1))

    col_spec = pl.BlockSpec((1, A_HEADS, A_DK, GLA_STEP_BB), lambda i: (i, 0, 0, 0))
    st_spec = pl.BlockSpec((GLA_STEP_BB, A_HEADS, A_DK, A_DV), lambda i: (i, 0, 0, 0))
    v0 = R_PROMPT // GLA_STEP_BB
    return pl.pallas_call(
        _gla_step_body, grid=(n_steps,),
        in_specs=[st_spec, col_spec, col_spec, col_spec,
                  pl.BlockSpec((GLA_STEP_BB, A_DV_TOT), lambda i: (v0 + i, 0))],
        out_specs=[st_spec, pl.BlockSpec((GLA_STEP_BB, A_DV_TOT), lambda i: (i, 0))],
        out_shape=[jax.ShapeDtypeStruct(s0.shape, F32),
                   jax.ShapeDtypeStruct((R_SAMPLE, A_DV_TOT), F32)],
        compiler_params=_params(1), name="gla_step")(s0, cols(q), cols(k), cols(la), v)


def _gla_out_body(o_ref, r_ref, x_ref, gh_ref, w_ref, y_ref):
    o = o_ref[...]
    heads = []
    for h in range(A_HEADS):
        heads.append(_rmsnorm(o[:, h * A_DV:(h + 1) * A_DV], gh_ref[...]))
    gated = jnp.concatenate(heads, axis=1) * _silu(r_ref[...])
    y_ref[...] = x_ref[...] + _dot(gated.astype(BF16), w_ref[...])


def _gla_out(o, r, x, g_head, w_out):
    return _row_call(_gla_out_body, [o, r, x], [g_head.reshape(1, A_DV), w_out.astype(BF16)],
                     [(D_MODEL, F32)], "gla_out")[0]


def _kv_proj_body(x_ref, g_ref, wk_ref, wv_ref, wf_ref, bf_ref, k_ref, v_ref, kb_ref, vb_ref, lf_ref):
    n = _rmsnorm(x_ref[...], g_ref[...]).astype(BF16)
    k = _dot(n, wk_ref[...])
    v = _dot(n, wv_ref[...])
    k_ref[...] = k
    v_ref[...] = v
    kb_ref[...] = k.astype(BF16)
    vb_ref[...] = v.astype(BF16)
    f = _dot(n, wf_ref[...])[:, :B_HEADS] + bf_ref[...]
    lf_ref[...] = _log_sigmoid(f)


def _kv_proj(x, g, w_in, b_f):
    w = w_in.astype(BF16)
    wf = jnp.pad(w[:, 2 * D_MODEL:], ((0, 0), (0, V7X_LANES - B_HEADS)))
    res = [g, w[:, :D_MODEL], w[:, D_MODEL:2 * D_MODEL], wf, b_f.reshape(1, B_HEADS)]
    outs = [(D_MODEL, F32), (D_MODEL, F32), (D_MODEL, BF16), (D_MODEL, BF16), (B_HEADS, F32)]
    return _row_call(_kv_proj_body, [x], res, outs, "kv_proj")


def _cumsum_body(lm_ref, lt_ref, cm_ref, ct_ref):
    def tri(n):
        return (lax.broadcasted_iota(jnp.int32, (n, n), 0) >=
                lax.broadcasted_iota(jnp.int32, (n, n), 1)).astype(F32)

    c_meta = _dot(tri(N_META), lt_ref[...], precision=HIGHEST)
    ct_ref[...] = c_meta
    tri_blk = tri(V7X_LANES)

    def step(j, carry):
        rows = pl.ds(pl.multiple_of(j * V7X_LANES, V7X_LANES), V7X_LANES)
        c = _dot(tri_blk, lm_ref[rows, :], precision=HIGHEST) + carry
        cm_ref[rows, :] = c
        return c[V7X_LANES - 1:V7X_LANES]

    lax.fori_loop(0, SEQ // V7X_LANES, step, c_meta[N_META - 1:N_META])


def _cumsum_prompt(lf):
    meta0 = R_MAIN // N_META
    return pl.pallas_call(
        _cumsum_body, grid=(BATCH,),
        in_specs=[pl.BlockSpec((SEQ, B_HEADS), lambda b: (b, 0)),
                  pl.BlockSpec((N_META, B_HEADS), lambda b: (meta0 + b, 0))],
        out_specs=[pl.BlockSpec((SEQ, B_HEADS), lambda b: (b, 0)),
                   pl.BlockSpec((N_META, B_HEADS), lambda b: (b, 0))],
        out_shape=[jax.ShapeDtypeStruct((R_MAIN, B_HEADS), F32),
                   jax.ShapeDtypeStruct((R_META, B_HEADS), F32)],
        compiler_params=_params(1), name="cumsum_logf")(lf, lf)


def _fox_q_body(x_ref, g_ref, w_ref, q_ref):
    n = _rmsnorm(x_ref[...], g_ref[...]).astype(BF16)
    q_ref[...] = (_dot(n, w_ref[...]) * (B_HD ** -0.5)).astype(BF16)


def _fox_q(x, g, w_q):
    return _row_call(_fox_q_body, [x], [g, w_q.astype(BF16)], [(D_MODEL, BF16)], "fox_q")[0]


def _fox_out_body(o_ref, x_ref, w_ref, y_ref):
    y_ref[...] = x_ref[...] + _dot(o_ref[...], w_ref[...])


def _fox_out(o, x, w_out):
    return _row_call(_fox_out_body, [o, x], [w_out.astype(BF16)], [(D_MODEL, F32)], "fox_out")[0]


def _fox_prompt_body(q_ref, qt_ref, k_ref, kt_ref, v_ref, vt_ref, cq_ref, cqt_ref, ck_ref, ckt_ref,
                     o_ref, ot_ref):
    i = pl.program_id(1)
    n_kblk = SEQ // FOX_TK
    causal = (lax.broadcasted_iota(jnp.int32, (FOX_TQ, FOX_TK), 0) >=
              lax.broadcasted_iota(jnp.int32, (FOX_TQ, FOX_TK), 1))
    diag = pl.multiple_of(i * FOX_TK, FOX_TK)

    for h in range(B_HEADS):
        hs = slice(h * B_HD, (h + 1) * B_HD)
        qh = q_ref[:, hs]
        cq = cq_ref[:, h:h + 1]
        k_meta, v_meta = kt_ref[:, hs], vt_ref[:, hs]
        c_meta = ckt_ref[0, h:h + 1, :]

        s = _dot_nt(qh, k_meta) + (cq - c_meta)
        m = jnp.max(s, axis=1, keepdims=True)
        p = jnp.exp(s - m)
        l = jnp.sum(p, axis=1, keepdims=True)
        acc = _dot(p.astype(BF16), v_meta)

        def block(off, ck_row, carry, masked):
            m, l, acc = carry
            s = _dot_nt(qh, k_ref[pl.ds(off, FOX_TK), hs]) + (cq - ck_row)
            if masked:
                s = jnp.where(causal, s, -jnp.inf)
            m_new = jnp.maximum(m, jnp.max(s, axis=1, keepdims=True))
            alpha = jnp.exp(m - m_new)
            p = jnp.exp(s - m_new)
            l = alpha * l + jnp.sum(p, axis=1, keepdims=True)
            acc = alpha * acc + _dot(p.astype(BF16), v_ref[pl.ds(off, FOX_TK), hs])
            return m_new, l, acc

        def full(j, carry):
            off = pl.multiple_of(j * FOX_TK, FOX_TK)
            return block(off, ck_ref[0, pl.ds(h * n_kblk + j, 1), :], carry, False)

        carry = lax.fori_loop(0, i, full, (m, l, acc))
        m, l, acc = block(diag, ck_ref[0, pl.ds(h * n_kblk + i, 1), :], carry, True)
        o_ref[:, hs] = (acc / l).astype(o_ref.dtype)

    @pl.when(i == 0)
    def _():
        tri = (lax.broadcasted_iota(jnp.int32, (N_META, N_META), 0) >=
               lax.broadcasted_iota(jnp.int32, (N_META, N_META), 1))
        for h in range(B_HEADS):
            hs = slice(h * B_HD, (h + 1) * B_HD)
            s = _dot_nt(qt_ref[:, hs], kt_ref[:, hs]) + (cqt_ref[:, h:h + 1] - ckt_ref[0, h:h + 1, :])
            s = jnp.where(tri, s, -jnp.inf)
            p = jnp.exp(s - jnp.max(s, axis=1, keepdims=True))
            o = _dot(p.astype(BF16), vt_ref[:, hs]) / jnp.sum(p, axis=1, keepdims=True)
            ot_ref[:, hs] = o.astype(ot_ref.dtype)


def _fox_prompt(q, kb, vb, c_main, c_meta):
    meta0 = R_MAIN // N_META
    n_qblk = SEQ // FOX_TQ
    n_kblk = SEQ // FOX_TK
    ck = jnp.transpose(c_main.reshape(BATCH, SEQ, B_HEADS), (0, 2, 1)).reshape(BATCH, B_HEADS * n_kblk, FOX_TK)
    ckt = jnp.transpose(c_meta.reshape(BATCH, N_META, B_HEADS), (0, 2, 1))
    main_rows = pl.BlockSpec((SEQ, D_MODEL), lambda b, i: (b, 0))
    meta_rows = pl.BlockSpec((N_META, D_MODEL), lambda b, i: (meta0 + b, 0))
    return pl.pallas_call(
        _fox_prompt_body, grid=(BATCH, n_qblk),
        in_specs=[pl.BlockSpec((FOX_TQ, D_MODEL), lambda b, i: (b * n_qblk + i, 0)), meta_rows,
                  main_rows, meta_rows, main_rows, meta_rows,
                  pl.BlockSpec((FOX_TQ, B_HEADS), lambda b, i: (b * n_qblk + i, 0)),
                  pl.BlockSpec((N_META, B_HEADS), lambda b, i: (b, 0)),
                  pl.BlockSpec((1, B_HEADS * n_kblk, FOX_TK), lambda b, i: (b, 0, 0)),
                  pl.BlockSpec((1, B_HEADS, N_META), lambda b, i: (b, 0, 0))],
        out_specs=[pl.BlockSpec((FOX_TQ, D_MODEL), lambda b, i: (b * n_qblk + i, 0)),
                   pl.BlockSpec((N_META, D_MODEL), lambda b, i: (b, 0))],
        out_shape=[jax.ShapeDtypeStruct((R_MAIN, D_MODEL), BF16),
                   jax.ShapeDtypeStruct((R_META, D_MODEL), BF16)],
        compiler_params=_params(2), name="fox_prompt")(q, q, kb, kb, vb, vb, c_main, c_meta, ck, ckt)


def _fox_decode_body(pt_ref, q_ref, kn_ref, vn_ref, lfn_ref, *refs):
    n = PAGES_PER_STEP
    k_refs, v_refs, lf_refs = refs[:n], refs[n:2 * n], refs[2 * n:3 * n]
    o_ref, m_ref, l_ref, acc_ref, suf_ref = refs[3 * n:]
    t = pl.program_id(1)

    lane_head = lax.broadcasted_iota(jnp.int32, (B_HEADS, D_MODEL), 1) // B_HD
    head = lax.broadcasted_iota(jnp.int32, (B_HEADS, D_MODEL), 0)
    expand_f32 = jnp.where(lane_head == head, 1.0, 0.0)
    expand = expand_f32.astype(BF16)
    q_rows = (q_ref[0].astype(F32) * expand_f32).astype(BF16)

    def expand_exact(row):
        return _dot(jnp.broadcast_to(row, (8, B_HEADS)), expand_f32, precision=HIGHEST)[0:1]

    @pl.when(t == 0)
    def _():
        k_new = jnp.broadcast_to(kn_ref[0], (8, D_MODEL)).astype(BF16)
        m_ref[...] = _dot_nt(k_new, q_rows)[0:1]
        l_ref[...] = jnp.ones((1, B_HEADS), F32)
        first = lax.broadcasted_iota(jnp.int32, (8, D_MODEL), 0) == 0
        acc_ref[...] = jnp.where(first, jnp.broadcast_to(vn_ref[0], (8, D_MODEL)), 0.0)
        suf_ref[...] = lfn_ref[0]

    after = (lax.broadcasted_iota(jnp.int32, (PAGE_SIZE, PAGE_SIZE), 1) >
             lax.broadcasted_iota(jnp.int32, (PAGE_SIZE, PAGE_SIZE), 0)).astype(F32)
    suf = suf_ref[...]
    logits = [None] * n
    for j in reversed(range(n)):
        lf = lf_refs[j][0]
        bias = _dot(after, lf, precision=HIGHEST) + suf
        suf = bias[0:1] + lf[0:1]
        logits[j] = _dot_nt(k_refs[j][0].astype(BF16), q_rows) + bias
    suf_ref[...] = suf

    s = jnp.concatenate(logits, axis=0)
    m_old = m_ref[...]
    m_new = jnp.maximum(m_old, jnp.max(s, axis=0, keepdims=True))
    alpha = jnp.exp(m_old - m_new)
    p = jnp.exp(s - m_new)
    l_ref[...] = alpha * l_ref[...] + jnp.sum(p, axis=0, keepdims=True)
    m_ref[...] = m_new
    acc = acc_ref[...] * expand_exact(alpha)
    for j in range(n):
        p_lanes = _dot(p[j * PAGE_SIZE:(j + 1) * PAGE_SIZE].astype(BF16), expand)
        acc = acc + (p_lanes * v_refs[j][0]).reshape(PAGE_SIZE // 8, 8, D_MODEL).sum(axis=0)
    acc_ref[...] = acc

    @pl.when(t == pl.num_programs(1) - 1)
    def _():
        o = jnp.sum(acc, axis=0, keepdims=True) / expand_exact(l_ref[...])
        o_ref[0] = o.astype(o_ref.dtype)


def _fox_decode(page_table, q, k, v, lf, cache_k, cache_v, cache_logf):
    n = PAGES_PER_STEP
    n_groups = N_PAGES // n
    n_phys = cache_k.shape[0]
    ck = cache_k.reshape(n_phys, PAGE_SIZE, D_MODEL)
    cv = cache_v.reshape(n_phys, PAGE_SIZE, D_MODEL)

    def new_rows(a):
        return a[R_PROMPT:].reshape(DEC_BATCH, 1, a.shape[1])

    def row_spec(width):
        return pl.BlockSpec((1, 1, width), lambda b, t, pt: (b, 0, 0))

    def page_spec(j, width):
        return pl.BlockSpec((1, PAGE_SIZE, width),
                            lambda b, t, pt: (pt[b, (n_groups - 1 - t) * n + j], 0, 0))

    in_specs = [row_spec(D_MODEL), row_spec(D_MODEL), row_spec(D_MODEL), row_spec(B_HEADS)]
    in_specs += [page_spec(j, D_MODEL) for j in range(n)]
    in_specs += [page_spec(j, D_MODEL) for j in range(n)]
    in_specs += [page_spec(j, B_HEADS) for j in range(n)]
    grid_spec = pltpu.PrefetchScalarGridSpec(
        num_scalar_prefetch=1, grid=(DEC_BATCH, n_groups), in_specs=in_specs,
        out_specs=pl.BlockSpec((1, 1, D_MODEL), lambda b, t, pt: (b, 0, 0)),
        scratch_shapes=[pltpu.VMEM((1, B_HEADS), F32), pltpu.VMEM((1, B_HEADS), F32),
                        pltpu.VMEM((8, D_MODEL), F32), pltpu.VMEM((1, B_HEADS), F32)])
    o = pl.pallas_call(
        _fox_decode_body, grid_spec=grid_spec,
        out_shape=jax.ShapeDtypeStruct((DEC_BATCH, 1, D_MODEL), BF16),
        compiler_params=_params(2), name="fox_decode")(
            page_table, new_rows(q), new_rows(k), new_rows(v), new_rows(lf),
            *([ck] * n), *([cv] * n), *([cache_logf] * n))
    return o.reshape(DEC_BATCH, D_MODEL)


def _final_norm_body(x_ref, g_ref, y_ref):
    y_ref[...] = _rmsnorm(x_ref[...], g_ref[...])


def _final_norm(x, g):
    return _row_call(_final_norm_body, [x], [g], [(D_MODEL, F32)], "final_norm")[0]


def kernel(x_prompt, x_sample, state_gla, cache_k, cache_v, cache_logf, page_table, meta_tokens,
           norm_ffn1, ffn1_w_gate, ffn1_w_up, ffn1_w_down, norm_mix, norm_ffn2,
           ffn2_w_gate, ffn2_w_up, ffn2_w_down, gla_w_in, gla_w_alpha2, gla_b_alpha, gla_g_head,
           gla_w_out, kv_norm, kv_w_in, kv_b_f, fox_w_q, fox_w_out, final_norm):
    def gain(g):
        return g.reshape(1, D_MODEL)

    meta = jnp.broadcast_to(meta_tokens[None], (BATCH, N_META, D_MODEL)).reshape(R_META, D_MODEL)
    x = jnp.concatenate([x_prompt.reshape(R_MAIN, D_MODEL), meta, x_sample.reshape(R_SAMPLE, D_MODEL)], axis=0)

    states_p, states_s = [], []
    k = v = kb = vb = lf = c_main = c_meta = None
    for l in range(DEPTH):
        x = _ffn(x, gain(norm_ffn1[l]), ffn1_w_gate[l].astype(BF16), ffn1_w_up[l].astype(BF16),
                 ffn1_w_down[l].astype(BF16))
        if l < N_A_LAYERS:
            q, kk, vv, r, la = _gla_proj(x, gain(norm_mix[l]), gla_w_in[l], gla_w_alpha2[l], gla_b_alpha[l])
            o_main, o_meta, st_p = _gla_scan(q, kk, vv, la)
            st_s, o_s = _gla_step(state_gla[l], q, kk, vv, la)
            states_p.append(st_p)
            states_s.append(st_s)
            x = _gla_out(jnp.concatenate([o_main, o_meta, o_s], axis=0), r, x, gla_g_head[l], gla_w_out[l])
        else:
            lb = l - N_A_LAYERS
            q = _fox_q(x, gain(norm_mix[l]), fox_w_q[lb])
            o_main, o_meta = _fox_prompt(q, kb, vb, c_main, c_meta)
            o_s = _fox_decode(page_table, q, k, v, lf, cache_k, cache_v, cache_logf)
            x = _fox_out(jnp.concatenate([o_main, o_meta, o_s], axis=0), x, fox_w_out[lb])
        x = _ffn(x, gain(norm_ffn2[l]), ffn2_w_gate[l].astype(BF16), ffn2_w_up[l].astype(BF16),
                 ffn2_w_down[l].astype(BF16))
        if l == N_A_LAYERS - 1:
            k, v, kb, vb, lf = _kv_proj(x, gain(kv_norm), kv_w_in, kv_b_f)
            c_main, c_meta = _cumsum_prompt(lf)
    y = _final_norm(x, gain(final_norm))

    def prompt_rows(a, width):
        return jnp.concatenate([a[R_MAIN:R_PROMPT].reshape(BATCH, N_META, width),
                                a[:R_MAIN].reshape(BATCH, SEQ, width)], axis=1)

    t_all = N_META + SEQ
    return (y[:R_MAIN].reshape(BATCH, SEQ, D_MODEL),
            y[R_PROMPT:].reshape(DEC_BATCH, 1, D_MODEL),
            jnp.stack(states_p, axis=0),
            prompt_rows(k, D_MODEL).reshape(BATCH, t_all, B_HEADS, B_HD),
            prompt_rows(v, D_MODEL).reshape(BATCH, t_all, B_HEADS, B_HD),
            prompt_rows(lf, B_HEADS),
            jnp.stack(states_s, axis=0),
            k[R_PROMPT:].reshape(DEC_BATCH, 1, B_HEADS, B_HD),
            v[R_PROMPT:].reshape(DEC_BATCH, 1, B_HEADS, B_HD),
            lf[R_PROMPT:].reshape(DEC_BATCH, 1, B_HEADS))
```

```python
import functools

import jax
import jax.numpy as jnp
from jax import lax
from jax.experimental import pallas as pl
from jax.experimental.pallas import tpu as pltpu

F32 = jnp.float32
BF16 = jnp.bfloat16
HIGHEST = lax.Precision.HIGHEST

D_MODEL = 1024
BATCH = 8
SEQ = 2048
DEPTH = 4
DEC_BATCH = 128
PAST_LEN = 2048
PAGE_SIZE = 128
N_PAGES = PAST_LEN // PAGE_SIZE
N_META = 16
N_A_LAYERS = 2
A_HEADS = 4
A_DK_TOT = 512
A_DV_TOT = 1024
A_DK = 128
A_DV = 256
A_GATE_RANK = 16
A_GATE_TAU = 16.0
B_HEADS = 16
B_HD = 64
D_FF = 2816
EPS = 1e-6

R_MAIN = BATCH * SEQ
R_META = BATCH * N_META
R_SAMPLE = DEC_BATCH
R_PROMPT = R_MAIN + R_META
R_ALL = R_PROMPT + R_SAMPLE

V7X_LANES = 128
V7X_VMEM_LIMIT = 56 * 1024 * 1024

ROW_TILE = 640
TAIL_TILE = R_META + R_SAMPLE
FF_CHUNK = 1408
GLA_CHUNK = 128
GLA_SUB = 16
GLA_HEADS_PER_STEP = 2
FOX_TQ = 256
FOX_TK = 256
PAGES_PER_STEP = 4
GLA_STEP_BB = 8


def _params(n_grid):
    return pltpu.CompilerParams(dimension_semantics=("arbitrary",) * n_grid,
                                vmem_limit_bytes=V7X_VMEM_LIMIT)


def _rmsnorm(x, g):
    return x * lax.rsqrt(jnp.mean(x * x, axis=-1, keepdims=True) + EPS) * g


def _log_sigmoid(z):
    return jnp.minimum(z, 0.0) - jnp.log1p(jnp.exp(-jnp.abs(z)))


def _silu(x):
    return x * jax.nn.sigmoid(x)


def _dot(a, b, precision=None):
    return jnp.dot(a, b, preferred_element_type=F32, precision=precision)


def _dot_nt(a, b):
    return lax.dot_general(a, b, (((1,), (1,)), ((), ())), preferred_element_type=F32)


def _dot_tn(a, b):
    return lax.dot_general(a, b, (((0,), (0,)), ((), ())), preferred_element_type=F32)


def _resident(a):
    nd = a.ndim
    return pl.BlockSpec(a.shape, lambda *_: (0,) * nd, pipeline_mode=pl.Buffered(1))


def _row_call(body, row_ins, res_ins, outs, name, tm=ROW_TILE):
    rows = row_ins[0].shape[0]
    in_specs = [pl.BlockSpec((tm, a.shape[1]), lambda i: (i, 0)) for a in row_ins]
    in_specs += [_resident(a) for a in res_ins]
    out_specs = [pl.BlockSpec((tm, c), lambda i: (i, 0)) for c, _ in outs]
    out_shape = [jax.ShapeDtypeStruct((rows, c), dt) for c, dt in outs]
    return pl.pallas_call(body, grid=(rows // tm,), in_specs=in_specs, out_specs=out_specs,
                          out_shape=out_shape, compiler_params=_params(1), name=name)(*row_ins, *res_ins)


def _split_row_call(body, main, tail, row_ins, res_ins, outs, name):
    tm = TAIL_TILE
    n_main = R_MAIN // tm
    in_specs = [pl.BlockSpec((tm, main.shape[1]), lambda i: (jnp.minimum(i, n_main - 1), 0)),
                pl.BlockSpec((tm, tail.shape[1]), lambda i: (0, 0))]
    in_specs += [pl.BlockSpec((tm, a.shape[1]), lambda i: (i, 0)) for a in row_ins]
    in_specs += [_resident(a) for a in res_ins]
    out_specs = [pl.BlockSpec((tm, c), lambda i: (i, 0)) for c, _ in outs]
    out_shape = [jax.ShapeDtypeStruct((R_ALL, c), dt) for c, dt in outs]
    return pl.pallas_call(body, grid=(R_ALL // tm,), in_specs=in_specs, out_specs=out_specs,
                          out_shape=out_shape, compiler_params=_params(1), name=name)(main, tail, *row_ins, *res_ins)


def _pick_tile(main_ref, tail_ref):
    is_tail = pl.program_id(0) == R_MAIN // TAIL_TILE
    return jnp.where(is_tail, tail_ref[...], main_ref[...])


def _ffn_body(x_ref, g_ref, wg_ref, wu_ref, wd_ref, o_ref):
    x = x_ref[...]
    n = _rmsnorm(x, g_ref[...]).astype(BF16)
    acc = jnp.zeros(x.shape, F32)
    for c in range(D_FF // FF_CHUNK):
        lo, hi = c * FF_CHUNK, (c + 1) * FF_CHUNK
        gate = _dot(n, wg_ref[:, lo:hi])
        up = _dot(n, wu_ref[:, lo:hi])
        acc = acc + _dot((_silu(gate) * up).astype(BF16), wd_ref[lo:hi, :])
    o_ref[...] = x + 0.5 * acc


def _ffn(x, g, wg, wu, wd):
    return _row_call(_ffn_body, [x], [g, wg, wu, wd], [(D_MODEL, F32)], "ffn")[0]


def _gla_proj_body(x_ref, g_ref, wq_ref, wk_ref, wv_ref, wr_ref, wa_ref, w2_ref, ba_ref,
                   q_ref, k_ref, v_ref, r_ref, la_ref):
    n = _rmsnorm(x_ref[...], g_ref[...]).astype(BF16)
    q_ref[...] = _dot(n, wq_ref[...]) * (A_DK ** -0.5)
    k_ref[...] = _dot(n, wk_ref[...])
    v_ref[...] = _dot(n, wv_ref[...])
    r_ref[...] = _dot(n, wr_ref[...])
    a = _dot(n, wa_ref[...]).astype(BF16)
    z = _dot(a, w2_ref[...]) + ba_ref[...]
    la_ref[...] = _log_sigmoid(z) * (1.0 / A_GATE_TAU)


def _gla_proj(x, g, w_in, w_alpha2, b_alpha):
    w = w_in.astype(BF16)
    c0, c1, c2, c3 = A_DK_TOT, 2 * A_DK_TOT, 2 * A_DK_TOT + A_DV_TOT, 2 * A_DK_TOT + 2 * A_DV_TOT
    wa = jnp.pad(w[:, c3:], ((0, 0), (0, V7X_LANES - A_GATE_RANK)))
    w2 = jnp.pad(w_alpha2.astype(BF16), ((0, V7X_LANES - A_GATE_RANK), (0, 0)))
    res = [g, w[:, :c0], w[:, c0:c1], w[:, c1:c2], w[:, c2:c3], wa, w2, b_alpha.reshape(1, A_DK_TOT)]
    outs = [(A_DK_TOT, F32), (A_DK_TOT, F32), (A_DV_TOT, F32), (A_DV_TOT, F32), (A_DK_TOT, F32)]
    return _row_call(_gla_proj_body, [x], res, outs, "gla_proj")


def _gla_chunk(q, k, v, la, st, ecat_ref, n_sub):
    c = GLA_SUB * n_sub
    row = lax.broadcasted_iota(jnp.int32, (c, c), 0)
    col = lax.broadcasted_iota(jnp.int32, (c, c), 1)
    causal = row >= col
    b = _dot(causal.astype(F32), la, precision=HIGHEST)
    a_rows = []
    for i in range(n_sub):
        lo = GLA_SUB * i
        b_i, q_i, k_i = b[lo:lo + GLA_SUB], q[lo:lo + GLA_SUB], k[lo:lo + GLA_SUB]
        pieces = []
        for s in range(GLA_SUB):
            decay = jnp.exp(jnp.minimum(b_i - b_i[s:s + 1], 0.0))
            pieces.append((q_i * k_i[s:s + 1] * decay).astype(BF16))
        a_i = _dot(jnp.concatenate(pieces, axis=1), ecat_ref[i])
        if i > 0:
            r_i = b[lo - 1:lo]
            q_s = (q_i * jnp.exp(b_i - r_i)).astype(BF16)
            k_s = (k[:lo] * jnp.exp(r_i - b[:lo])).astype(BF16)
            k_s = jnp.concatenate([k_s, jnp.zeros((c - lo, A_DK), BF16)], axis=0)
            a_i = a_i + _dot_nt(q_s, k_s)
        a_rows.append(a_i)
    a = jnp.concatenate(a_rows, axis=0)[:, :c]
    a = jnp.where(causal, a, 0.0)
    vb = v.astype(BF16)
    o = _dot(a.astype(BF16), vb)
    b_end = b[c - 1:c]
    k_d = (k * jnp.exp(b_end - b)).astype(BF16)
    st_new = _dot_tn(vb, k_d)
    if st is not None:
        o = o + _dot_nt((q * jnp.exp(b)).astype(BF16), st.astype(BF16))
        st_new = st_new + st * jnp.exp(b_end)
    return o, st_new


def _gla_scan_body(q_ref, k_ref, v_ref, la_ref, qt_ref, kt_ref, vt_ref, lat_ref, ecat_ref,
                   o_ref, ot_ref, s_ref, st_ref):
    heads = [(slice(g * A_DK, (g + 1) * A_DK), slice(g * A_DV, (g + 1) * A_DV)) for g in range(GLA_HEADS_PER_STEP)]
    for g, (kl, vl) in enumerate(heads):
        o_meta, st = _gla_chunk(qt_ref[:, kl], kt_ref[:, kl], vt_ref[:, vl], lat_ref[:, kl], None, ecat_ref, 1)
        ot_ref[:, vl] = o_meta
        st_ref[g] = st

    def step(j, carry):
        rows = pl.ds(pl.multiple_of(j * GLA_CHUNK, GLA_CHUNK), GLA_CHUNK)
        for g, (kl, vl) in enumerate(heads):
            o, st_new = _gla_chunk(q_ref[rows, kl], k_ref[rows, kl], v_ref[rows, vl], la_ref[rows, kl],
                                   st_ref[g], ecat_ref, GLA_CHUNK // GLA_SUB)
            o_ref[rows, vl] = o
            st_ref[g] = st_new
        return carry

    lax.fori_loop(0, SEQ // GLA_CHUNK, step, 0)
    for g in range(GLA_HEADS_PER_STEP):
        s_ref[0, g] = st_ref[g].T


def _gla_ecat():
    n_sub = GLA_CHUNK // GLA_SUB
    key = lax.broadcasted_iota(jnp.int32, (n_sub, GLA_SUB * A_DK, GLA_CHUNK), 1) // A_DK
    sub = lax.broadcasted_iota(jnp.int32, (n_sub, GLA_SUB * A_DK, GLA_CHUNK), 0)
    col = lax.broadcasted_iota(jnp.int32, (n_sub, GLA_SUB * A_DK, GLA_CHUNK), 2)
    return (col == GLA_SUB * sub + key).astype(BF16)


def _gla_scan(q, k, v, la):
    meta0 = R_MAIN // N_META
    n = GLA_HEADS_PER_STEP
    main_k = pl.BlockSpec((SEQ, n * A_DK), lambda b, h: (b, h))
    main_v = pl.BlockSpec((SEQ, n * A_DV), lambda b, h: (b, h))
    meta_k = pl.BlockSpec((N_META, n * A_DK), lambda b, h: (meta0 + b, h))
    meta_v = pl.BlockSpec((N_META, n * A_DV), lambda b, h: (meta0 + b, h))
    ecat = _gla_ecat()
    return pl.pallas_call(
        _gla_scan_body, grid=(BATCH, A_HEADS // n),
        in_specs=[main_k, main_k, main_v, main_k, meta_k, meta_k, meta_v, meta_k, _resident(ecat)],
        out_specs=[pl.BlockSpec((SEQ, n * A_DV), lambda b, h: (b, h)),
                   pl.BlockSpec((N_META, n * A_DV), lambda b, h: (b, h)),
                   pl.BlockSpec((1, n, A_DK, A_DV), lambda b, h: (b, h, 0, 0))],
        out_shape=[jax.ShapeDtypeStruct((R_MAIN, A_DV_TOT), F32),
                   jax.ShapeDtypeStruct((R_META, A_DV_TOT), F32),
                   jax.ShapeDtypeStruct((BATCH, A_HEADS, A_DK, A_DV), F32)],
        scratch_shapes=[pltpu.VMEM((n, A_DV, A_DK), F32)],
        compiler_params=_params(2), name="gla_scan")(q, k, v, la, q, k, v, la, ecat)


def _gla_step_body(s0_ref, qt_ref, kt_ref, lat_ref, v_ref, s1_ref, o_ref):
    for bi in range(GLA_STEP_BB):
        for h in range(A_HEADS):
            cols = slice(h * A_DV, (h + 1) * A_DV)
            k_col = kt_ref[0, h, :, bi:bi + 1]
            q_col = qt_ref[0, h, :, bi:bi + 1]
            decay = jnp.exp(lat_ref[0, h, :, bi:bi + 1])
            s1 = decay * s0_ref[bi, h] + k_col * v_ref[bi:bi + 1, cols]
            s1_ref[bi, h] = s1
            o_ref[bi:bi + 1, cols] = jnp.sum(q_col * s1, axis=0, keepdims=True)


def _gla_step(s0, q, k, v, la):
    n_steps = DEC_BATCH // GLA_STEP_BB

    def cols(a):
        a = a[R_PROMPT:].reshape(n_steps, GLA_STEP_BB, A_HEADS, A_DK)
        return jnp.transpose(a, (0, 2, 3, 1))

    col_spec = pl.BlockSpec((1, A_HEADS, A_DK, GLA_STEP_BB), lambda i: (i, 0, 0, 0))
    st_spec = pl.BlockSpec((GLA_STEP_BB, A_HEADS, A_DK, A_DV), lambda i: (i, 0, 0, 0))
    v0 = R_PROMPT // GLA_STEP_BB
    return pl.pallas_call(
        _gla_step_body, grid=(n_steps,),
        in_specs=[st_spec, col_spec, col_spec, col_spec,
                  pl.BlockSpec((GLA_STEP_BB, A_DV_TOT), lambda i: (v0 + i, 0))],
        out_specs=[st_spec, pl.BlockSpec((GLA_STEP_BB, A_DV_TOT), lambda i: (i, 0))],
        out_shape=[jax.ShapeDtypeStruct(s0.shape, F32),
                   jax.ShapeDtypeStruct((R_SAMPLE, A_DV_TOT), F32)],
        compiler_params=_params(1), name="gla_step")(s0, cols(q), cols(k), cols(la), v)


def _gla_out_body(om_ref, ot_ref, r_ref, x_ref, gh_ref, w_ref, y_ref):
    o = _pick_tile(om_ref, ot_ref)
    heads = []
    for h in range(A_HEADS):
        heads.append(_rmsnorm(o[:, h * A_DV:(h + 1) * A_DV], gh_ref[...]))
    gated = jnp.concatenate(heads, axis=1) * _silu(r_ref[...])
    y_ref[...] = x_ref[...] + _dot(gated.astype(BF16), w_ref[...])


def _gla_out(o_main, o_tail, r, x, g_head, w_out):
    return _split_row_call(_gla_out_body, o_main, o_tail, [r, x], [g_head.reshape(1, A_DV), w_out.astype(BF16)],
                           [(D_MODEL, F32)], "gla_out")[0]


def _kv_proj_body(x_ref, g_ref, wk_ref, wv_ref, wf_ref, bf_ref, k_ref, v_ref, kb_ref, vb_ref, lf_ref):
    n = _rmsnorm(x_ref[...], g_ref[...]).astype(BF16)
    k = _dot(n, wk_ref[...])
    v = _dot(n, wv_ref[...])
    k_ref[...] = k
    v_ref[...] = v
    kb_ref[...] = k.astype(BF16)
    vb_ref[...] = v.astype(BF16)
    f = _dot(n, wf_ref[...])[:, :B_HEADS] + bf_ref[...]
    lf_ref[...] = _log_sigmoid(f)


def _kv_proj(x, g, w_in, b_f):
    w = w_in.astype(BF16)
    wf = jnp.pad(w[:, 2 * D_MODEL:], ((0, 0), (0, V7X_LANES - B_HEADS)))
    res = [g, w[:, :D_MODEL], w[:, D_MODEL:2 * D_MODEL], wf, b_f.reshape(1, B_HEADS)]
    outs = [(D_MODEL, F32), (D_MODEL, F32), (D_MODEL, BF16), (D_MODEL, BF16), (B_HEADS, F32)]
    return _row_call(_kv_proj_body, [x], res, outs, "kv_proj")


def _cumsum_body(lm_ref, lt_ref, cm_ref, ct_ref):
    def tri(n):
        return (lax.broadcasted_iota(jnp.int32, (n, n), 0) >=
                lax.broadcasted_iota(jnp.int32, (n, n), 1)).astype(F32)

    c_meta = _dot(tri(N_META), lt_ref[...], precision=HIGHEST)
    ct_ref[...] = c_meta
    tri_blk = tri(V7X_LANES)

    def step(j, carry):
        rows = pl.ds(pl.multiple_of(j * V7X_LANES, V7X_LANES), V7X_LANES)
        c = _dot(tri_blk, lm_ref[rows, :], precision=HIGHEST) + carry
        cm_ref[rows, :] = c
        return c[V7X_LANES - 1:V7X_LANES]

    lax.fori_loop(0, SEQ // V7X_LANES, step, c_meta[N_META - 1:N_META])


def _cumsum_prompt(lf):
    meta0 = R_MAIN // N_META
    return pl.pallas_call(
        _cumsum_body, grid=(BATCH,),
        in_specs=[pl.BlockSpec((SEQ, B_HEADS), lambda b: (b, 0)),
                  pl.BlockSpec((N_META, B_HEADS), lambda b: (meta0 + b, 0))],
        out_specs=[pl.BlockSpec((SEQ, B_HEADS), lambda b: (b, 0)),
                   pl.BlockSpec((N_META, B_HEADS), lambda b: (b, 0))],
        out_shape=[jax.ShapeDtypeStruct((R_MAIN, B_HEADS), F32),
                   jax.ShapeDtypeStruct((R_META, B_HEADS), F32)],
        compiler_params=_params(1), name="cumsum_logf")(lf, lf)


def _fox_q_body(x_ref, g_ref, w_ref, q_ref):
    n = _rmsnorm(x_ref[...], g_ref[...]).astype(BF16)
    q_ref[...] = (_dot(n, w_ref[...]) * (B_HD ** -0.5)).astype(BF16)


def _fox_q_sample(x, g, w_q):
    return _row_call(_fox_q_body, [x[R_PROMPT:]], [g, w_q.astype(BF16)], [(D_MODEL, BF16)], "fox_q_sample",
                     tm=R_SAMPLE)[0]


AUG_LANES = V7X_LANES
AUG_WIDTH = B_HEADS * AUG_LANES
AUG_K_BIAS = B_HD
AUG_Q_BIAS = B_HD + 3


def _split3(c):
    hi = c.astype(BF16)
    r1 = c - hi.astype(F32)
    mid = r1.astype(BF16)
    lo = (r1 - mid.astype(F32)).astype(BF16)
    return hi, mid, lo


def _aug_consts(bias_lane, ones_lane, sign):
    lane = lax.broadcasted_iota(jnp.int32, (3, B_HEADS, AUG_WIDTH), 2)
    head = lax.broadcasted_iota(jnp.int32, (3, B_HEADS, AUG_WIDTH), 1)
    part = lax.broadcasted_iota(jnp.int32, (3, B_HEADS, AUG_WIDTH), 0)
    place = jnp.where(lane == head * AUG_LANES + bias_lane + part, sign, 0.0).astype(BF16)
    in_group = lax.broadcasted_iota(jnp.int32, (1, AUG_WIDTH), 1) % AUG_LANES
    ones = jnp.where((in_group >= ones_lane) & (in_group < ones_lane + 3), 1.0, 0.0).astype(F32)
    return place, ones


def _add_bias_lanes(x, c, place_ref, ones_ref):
    hi, mid, lo = _split3(c)
    return x + _dot(hi, place_ref[0]) + _dot(mid, place_ref[1]) + _dot(lo, place_ref[2]) + ones_ref[...]


def _fox_qaug_body(x_ref, c_ref, g_ref, w_ref, place_ref, ones_ref, q_ref):
    n = _rmsnorm(x_ref[...], g_ref[...]).astype(BF16)
    q = _dot(n, w_ref[...]) * (B_HD ** -0.5)
    q_ref[...] = _add_bias_lanes(q, c_ref[...], place_ref, ones_ref).astype(BF16)


def _fox_qaug(x, c, g, w_q):
    w = jnp.pad(w_q.astype(BF16).reshape(D_MODEL, B_HEADS, B_HD), ((0, 0), (0, 0), (0, AUG_LANES - B_HD)))
    place, ones = _aug_consts(AUG_Q_BIAS, AUG_K_BIAS, 1.0)
    return _row_call(_fox_qaug_body, [x, c], [g, w.reshape(D_MODEL, AUG_WIDTH), place, ones],
                     [(AUG_WIDTH, BF16)], "fox_qaug")[0]


def _fox_kaug_body(k_ref, c_ref, spread_ref, place_ref, ones_ref, o_ref):
    k = _dot(k_ref[...], spread_ref[...])
    o_ref[...] = _add_bias_lanes(k, c_ref[...], place_ref, ones_ref).astype(BF16)


def _fox_kaug(kb, c):
    src = lax.broadcasted_iota(jnp.int32, (D_MODEL, AUG_WIDTH), 0)
    dst = lax.broadcasted_iota(jnp.int32, (D_MODEL, AUG_WIDTH), 1)
    spread = (dst == (src // B_HD) * AUG_LANES + src % B_HD).astype(BF16)
    place, ones = _aug_consts(AUG_K_BIAS, AUG_Q_BIAS, -1.0)
    return _row_call(_fox_kaug_body, [kb, c], [spread, place, ones], [(AUG_WIDTH, BF16)], "fox_kaug")[0]


def _fox_out_body(om_ref, ot_ref, x_ref, w_ref, y_ref):
    y_ref[...] = x_ref[...] + _dot(_pick_tile(om_ref, ot_ref), w_ref[...])


def _fox_out(o_main, o_tail, x, w_out):
    return _split_row_call(_fox_out_body, o_main, o_tail, [x], [w_out.astype(BF16)], [(D_MODEL, F32)],
                           "fox_out")[0]


def _fox_prompt_body(q_ref, qt_ref, k_ref, kt_ref, vT_ref, vTt_ref, vt_ref, o_ref, ot_ref,
                     m_ref, l_ref, a_ref, acc_ref, s_ref, p_ref):
    i = pl.program_id(1)
    visible = (lax.broadcasted_iota(jnp.int32, (FOX_TK, FOX_TQ), 0) <=
               lax.broadcasted_iota(jnp.int32, (FOX_TK, FOX_TQ), 1))

    m_ref[...] = jnp.full(m_ref.shape, -jnp.inf, F32)
    l_ref[...] = jnp.zeros(l_ref.shape, F32)
    acc_ref[...] = jnp.zeros(acc_ref.shape, F32)

    def attend(n_keys, keys_of, values_of, masked):
        for h in range(B_HEADS):
            grp = slice(h * AUG_LANES, (h + 1) * AUG_LANES)
            s_ref[h, :n_keys] = _dot_nt(keys_of(grp), q_ref[:, grp])
        for h in range(B_HEADS):
            for c in range(FOX_TQ // V7X_LANES):
                qs = slice(c * V7X_LANES, (c + 1) * V7X_LANES)
                s = s_ref[h, :n_keys, qs]
                if masked:
                    s = jnp.where(visible[:, qs], s, -jnp.inf)
                m_old = m_ref[h:h + 1, qs]
                m_new = jnp.maximum(m_old, jnp.max(s, axis=0, keepdims=True))
                alpha = jnp.exp(m_old - m_new)
                p = jnp.exp(s - m_new)
                p_ref[h, :n_keys, qs] = p.astype(BF16)
                m_ref[h:h + 1, qs] = m_new
                a_ref[h:h + 1, qs] = alpha
                l_ref[h:h + 1, qs] = alpha * l_ref[h:h + 1, qs] + jnp.sum(p, axis=0, keepdims=True)
        for h in range(B_HEADS):
            rows = slice(h * B_HD, (h + 1) * B_HD)
            acc_ref[rows, :] = a_ref[h:h + 1, :] * acc_ref[rows, :] + _dot(values_of(rows), p_ref[h, :n_keys])

    def key_block(j, masked):
        keys = pl.ds(pl.multiple_of(j * FOX_TK, FOX_TK), FOX_TK)
        attend(FOX_TK, lambda grp: k_ref[keys, grp], lambda rows: vT_ref[0, j, rows, :], masked)

    def full_block(j, carry):
        key_block(j, False)
        return carry

    attend(N_META, lambda grp: kt_ref[:, grp], lambda rows: vTt_ref[0, rows, :], False)
    lax.fori_loop(0, i, full_block, 0)
    key_block(i, True)

    for h in range(B_HEADS):
        rows = slice(h * B_HD, (h + 1) * B_HD)
        acc_ref[rows, :] = acc_ref[rows, :] / l_ref[h:h + 1, :]
    o_ref[...] = acc_ref[...].T.astype(o_ref.dtype)

    @pl.when(i == 0)
    def _():
        tri = (lax.broadcasted_iota(jnp.int32, (N_META, N_META), 0) >=
               lax.broadcasted_iota(jnp.int32, (N_META, N_META), 1))
        for h in range(B_HEADS):
            grp = slice(h * AUG_LANES, (h + 1) * AUG_LANES)
            hs = slice(h * B_HD, (h + 1) * B_HD)
            s = jnp.where(tri, _dot_nt(qt_ref[:, grp], kt_ref[:, grp]), -jnp.inf)
            p = jnp.exp(s - jnp.max(s, axis=1, keepdims=True))
            o = _dot(p.astype(BF16), vt_ref[:, hs]) / jnp.sum(p, axis=1, keepdims=True)
            ot_ref[:, hs] = o.astype(ot_ref.dtype)


def _fox_values_t(vb):
    vT = jnp.transpose(vb[:R_MAIN].reshape(BATCH, SEQ // FOX_TK, FOX_TK, D_MODEL), (0, 1, 3, 2))
    vTt = jnp.transpose(vb[R_MAIN:R_PROMPT].reshape(BATCH, N_META, D_MODEL), (0, 2, 1))
    return vT, vTt


def _fox_prompt(qaug, kaug, vb, vT, vTt):
    meta0 = R_MAIN // N_META
    n_qblk = SEQ // FOX_TQ
    n_kblk = SEQ // FOX_TK
    return pl.pallas_call(
        _fox_prompt_body, grid=(BATCH, n_qblk),
        in_specs=[pl.BlockSpec((FOX_TQ, AUG_WIDTH), lambda b, i: (b * n_qblk + i, 0)),
                  pl.BlockSpec((N_META, AUG_WIDTH), lambda b, i: (meta0 + b, 0)),
                  pl.BlockSpec((SEQ, AUG_WIDTH), lambda b, i: (b, 0)),
                  pl.BlockSpec((N_META, AUG_WIDTH), lambda b, i: (meta0 + b, 0)),
                  pl.BlockSpec((1, n_kblk, D_MODEL, FOX_TK), lambda b, i: (b, 0, 0, 0)),
                  pl.BlockSpec((1, D_MODEL, N_META), lambda b, i: (b, 0, 0)),
                  pl.BlockSpec((N_META, D_MODEL), lambda b, i: (meta0 + b, 0))],
        out_specs=[pl.BlockSpec((FOX_TQ, D_MODEL), lambda b, i: (b * n_qblk + i, 0)),
                   pl.BlockSpec((N_META, D_MODEL), lambda b, i: (b, 0))],
        out_shape=[jax.ShapeDtypeStruct((R_MAIN, D_MODEL), BF16),
                   jax.ShapeDtypeStruct((R_META, D_MODEL), BF16)],
        scratch_shapes=[pltpu.VMEM((B_HEADS, FOX_TQ), F32), pltpu.VMEM((B_HEADS, FOX_TQ), F32),
                        pltpu.VMEM((B_HEADS, FOX_TQ), F32), pltpu.VMEM((D_MODEL, FOX_TQ), F32),
                        pltpu.VMEM((B_HEADS, FOX_TK, FOX_TQ), F32),
                        pltpu.VMEM((B_HEADS, FOX_TK, FOX_TQ), BF16)],
        compiler_params=_params(2), name="fox_prompt")(qaug, qaug, kaug, kaug, vT, vTt, vb)


def _cache_pack_body(pt_ref, lfn_ref, *refs):
    n = PAGES_PER_STEP
    k_refs, v_refs, lf_refs = refs[:n], refs[n:2 * n], refs[2 * n:3 * n]
    kc_ref, vc_ref, bias_ref, suf_ref = refs[3 * n:]

    @pl.when(pl.program_id(1) == 0)
    def _():
        suf_ref[...] = lfn_ref[0]

    after = (lax.broadcasted_iota(jnp.int32, (PAGE_SIZE, PAGE_SIZE), 1) >
             lax.broadcasted_iota(jnp.int32, (PAGE_SIZE, PAGE_SIZE), 0)).astype(F32)
    suf = suf_ref[...]
    for j in reversed(range(n)):
        rows = slice(j * PAGE_SIZE, (j + 1) * PAGE_SIZE)
        for pair in range(B_HEADS // 2):
            lanes = slice(pair * 2 * B_HD, (pair + 1) * 2 * B_HD)
            for src, dst in ((k_refs[j], kc_ref), (v_refs[j], vc_ref)):
                even = src[0, pl.ds(2 * pair, PAGE_SIZE, stride=B_HEADS), :]
                odd = src[0, pl.ds(2 * pair + 1, PAGE_SIZE, stride=B_HEADS), :]
                dst[0, rows, lanes] = jnp.concatenate([even, odd], axis=1).astype(BF16)
        lf = lf_refs[j][0]
        bias = _dot(after, lf, precision=HIGHEST) + suf
        suf = bias[0:1] + lf[0:1]
        bias_ref[0, rows, :] = bias
    suf_ref[...] = suf


def _cache_pack(page_table, lf, cache_k, cache_v, cache_logf):
    n = PAGES_PER_STEP
    n_groups = N_PAGES // n

    def group(t):
        return n_groups - 1 - t

    def page_spec(j, tail, rows):
        return pl.BlockSpec((1, rows) + tail,
                            lambda b, t, pt: (pt[b, group(t) * n + j],) + (0,) * (1 + len(tail)))

    def out_spec(width):
        return pl.BlockSpec((1, n * PAGE_SIZE, width), lambda b, t, pt: (b, group(t), 0))

    in_specs = [pl.BlockSpec((1, 1, B_HEADS), lambda b, t, pt: (b, 0, 0))]
    in_specs += [page_spec(j, (B_HD,), PAGE_SIZE * B_HEADS) for j in range(n)]
    in_specs += [page_spec(j, (B_HD,), PAGE_SIZE * B_HEADS) for j in range(n)]
    in_specs += [page_spec(j, (B_HEADS,), PAGE_SIZE) for j in range(n)]
    grid_spec = pltpu.PrefetchScalarGridSpec(
        num_scalar_prefetch=1, grid=(DEC_BATCH, n_groups), in_specs=in_specs,
        out_specs=[out_spec(D_MODEL), out_spec(D_MODEL), out_spec(B_HEADS)],
        scratch_shapes=[pltpu.VMEM((1, B_HEADS), F32)])
    lf_new = lf[R_PROMPT:].reshape(DEC_BATCH, 1, B_HEADS)

    def rows_of(cache):
        return cache.reshape(cache.shape[0], PAGE_SIZE * B_HEADS, B_HD)

    return pl.pallas_call(
        _cache_pack_body, grid_spec=grid_spec,
        out_shape=[jax.ShapeDtypeStruct((DEC_BATCH, PAST_LEN, D_MODEL), BF16),
                   jax.ShapeDtypeStruct((DEC_BATCH, PAST_LEN, D_MODEL), BF16),
                   jax.ShapeDtypeStruct((DEC_BATCH, PAST_LEN, B_HEADS), F32)],
        compiler_params=_params(2), name="cache_pack")(
            page_table, lf_new, *([rows_of(cache_k)] * n), *([rows_of(cache_v)] * n), *([cache_logf] * n))


def _fox_decode_body(q_ref, kn_ref, vn_ref, kc_ref, vc_ref, bias_ref, o_ref):
    lane_head = lax.broadcasted_iota(jnp.int32, (B_HEADS, D_MODEL), 1) // B_HD
    head = lax.broadcasted_iota(jnp.int32, (B_HEADS, D_MODEL), 0)
    own = jnp.where(lane_head == head, 1.0, 0.0)
    q_rows = q_ref[0].astype(F32) * own
    s = _dot_nt(q_rows.astype(BF16), kc_ref[0]) + bias_ref[0]
    k_new = kn_ref[0].astype(BF16).astype(F32)
    s_new = jnp.sum(q_rows * k_new, axis=1, keepdims=True)
    m = jnp.maximum(jnp.max(s, axis=1, keepdims=True), s_new)
    p = jnp.exp(s - m)
    p_new = jnp.exp(s_new - m)
    l = jnp.sum(p, axis=1, keepdims=True) + p_new
    out = _dot(p.astype(BF16), vc_ref[0]) + p_new * vn_ref[0]
    o_ref[0] = jnp.sum(out * own / l, axis=0, keepdims=True).astype(o_ref.dtype)


def _fox_decode(q_s, k, v, kc, vc, bias_t):
    def new_rows(a):
        return a[R_PROMPT:].reshape(DEC_BATCH, 1, D_MODEL)

    row = pl.BlockSpec((1, 1, D_MODEL), lambda b: (b, 0, 0))
    dense = pl.BlockSpec((1, PAST_LEN, D_MODEL), lambda b: (b, 0, 0))
    o = pl.pallas_call(
        _fox_decode_body, grid=(DEC_BATCH,),
        in_specs=[row, row, row, dense, dense, pl.BlockSpec((1, B_HEADS, PAST_LEN), lambda b: (b, 0, 0))],
        out_specs=row, out_shape=jax.ShapeDtypeStruct((DEC_BATCH, 1, D_MODEL), BF16),
        compiler_params=_params(1), name="fox_decode")(
            q_s.reshape(DEC_BATCH, 1, D_MODEL), new_rows(k), new_rows(v), kc, vc, bias_t)
    return o.reshape(DEC_BATCH, D_MODEL)


def _final_norm_body(x_ref, g_ref, y_ref):
    y_ref[...] = _rmsnorm(x_ref[...], g_ref[...])


def _final_norm(x, g):
    return _row_call(_final_norm_body, [x], [g], [(D_MODEL, F32)], "final_norm")[0]


def kernel(x_prompt, x_sample, state_gla, cache_k, cache_v, cache_logf, page_table, meta_tokens,
           norm_ffn1, ffn1_w_gate, ffn1_w_up, ffn1_w_down, norm_mix, norm_ffn2,
           ffn2_w_gate, ffn2_w_up, ffn2_w_down, gla_w_in, gla_w_alpha2, gla_b_alpha, gla_g_head,
           gla_w_out, kv_norm, kv_w_in, kv_b_f, fox_w_q, fox_w_out, final_norm):
    def gain(g):
        return g.reshape(1, D_MODEL)

    meta = jnp.broadcast_to(meta_tokens[None], (BATCH, N_META, D_MODEL)).reshape(R_META, D_MODEL)
    x = jnp.concatenate([x_prompt.reshape(R_MAIN, D_MODEL), meta, x_sample.reshape(R_SAMPLE, D_MODEL)], axis=0)

    states_p, states_s = [], []
    k = v = vb = vT = vTt = c_rows = kaug = kc = vc = bias_t = None
    for l in range(DEPTH):
        x = _ffn(x, gain(norm_ffn1[l]), ffn1_w_gate[l].astype(BF16), ffn1_w_up[l].astype(BF16),
                 ffn1_w_down[l].astype(BF16))
        if l < N_A_LAYERS:
            q, kk, vv, r, la = _gla_proj(x, gain(norm_mix[l]), gla_w_in[l], gla_w_alpha2[l], gla_b_alpha[l])
            o_main, o_meta, st_p = _gla_scan(q, kk, vv, la)
            st_s, o_s = _gla_step(state_gla[l], q, kk, vv, la)
            states_p.append(st_p)
            states_s.append(st_s)
            x = _gla_out(o_main, jnp.concatenate([o_meta, o_s], axis=0), r, x, gla_g_head[l], gla_w_out[l])
        else:
            lb = l - N_A_LAYERS
            qaug = _fox_qaug(x, c_rows, gain(norm_mix[l]), fox_w_q[lb])
            o_main, o_meta = _fox_prompt(qaug, kaug, vb, vT, vTt)
            q_s = _fox_q_sample(x, gain(norm_mix[l]), fox_w_q[lb])
            o_s = _fox_decode(q_s, k, v, kc, vc, bias_t)
            x = _fox_out(o_main, jnp.concatenate([o_meta, o_s], axis=0), x, fox_w_out[lb])
        x = _ffn(x, gain(norm_ffn2[l]), ffn2_w_gate[l].astype(BF16), ffn2_w_up[l].astype(BF16),
                 ffn2_w_down[l].astype(BF16))
        if l == N_A_LAYERS - 1:
            k, v, kb, vb, lf = _kv_proj(x, gain(kv_norm), kv_w_in, kv_b_f)
            c_main, c_meta = _cumsum_prompt(lf)
            c_rows = jnp.concatenate([c_main, c_meta, jnp.zeros((R_SAMPLE, B_HEADS), F32)], axis=0)
            kaug = _fox_kaug(kb, c_rows)
            vT, vTt = _fox_values_t(vb)
            kc, vc, bias = _cache_pack(page_table, lf, cache_k, cache_v, cache_logf)
            bias_t = jnp.transpose(bias, (0, 2, 1))
    y =_final_norm(x, gain(final_norm))

    def prompt_rows(a, width):
        return jnp.concatenate([a[R_MAIN:R_PROMPT].reshape(BATCH, N_META, width),
                                a[:R_MAIN].reshape(BATCH, SEQ, width)], axis=1)

    t_all = N_META + SEQ
    return (y[:R_MAIN].reshape(BATCH, SEQ, D_MODEL),
            y[R_PROMPT:].reshape(DEC_BATCH, 1, D_MODEL),
            jnp.stack(states_p, axis=0),
            prompt_rows(k, D_MODEL).reshape(BATCH, t_all, B_HEADS, B_HD),
            prompt_rows(v, D_MODEL).reshape(BATCH, t_all, B_HEADS, B_HD),
            prompt_rows(lf, B_HEADS),
            jnp.stack(states_s, axis=0),
            k[R_PROMPT:].reshape(DEC_BATCH, 1, B_HEADS, B_HD),
            v[R_PROMPT:].reshape(DEC_BATCH, 1, B_HEADS, B_HD),
            lf[R_PROMPT:].reshape(DEC_BATCH, 1, B_HEADS))
```

```python
import functools

import jax
import jax.numpy as jnp
from jax import lax
from jax.experimental import pallas as pl
from jax.experimental.pallas import tpu as pltpu

F32 = jnp.float32
BF16 = jnp.bfloat16
HIGHEST = lax.Precision.HIGHEST

D_MODEL = 1024
BATCH = 8
SEQ = 2048
DEPTH = 4
DEC_BATCH = 128
PAST_LEN = 2048
PAGE_SIZE = 128
N_PAGES = PAST_LEN // PAGE_SIZE
N_META = 16
N_A_LAYERS = 2
A_HEADS = 4
A_DK_TOT = 512
A_DV_TOT = 1024
A_DK = 128
A_DV = 256
A_GATE_RANK = 16
A_GATE_TAU = 16.0
B_HEADS = 16
B_HD = 64
D_FF = 2816
EPS = 1e-6

R_MAIN = BATCH * SEQ
R_META = BATCH * N_META
R_SAMPLE = DEC_BATCH
R_PROMPT = R_MAIN + R_META
R_ALL = R_PROMPT + R_SAMPLE

V7X_LANES = 128
V7X_VMEM_LIMIT = 56 * 1024 * 1024

ROW_TILE = 640
TAIL_TILE = R_META + R_SAMPLE
FF_CHUNK = 1408
GLA_CHUNK = 128
GLA_SUB = 16
GLA_HEADS_PER_STEP = 2
FOX_TQ = 256
FOX_TK = 256
PAGES_PER_STEP = 8
GLA_STEP_BB = 8


def _params(n_grid):
    return pltpu.CompilerParams(dimension_semantics=("arbitrary",) * n_grid,
                                vmem_limit_bytes=V7X_VMEM_LIMIT)


def _rmsnorm(x, g):
    return x * lax.rsqrt(jnp.mean(x * x, axis=-1, keepdims=True) + EPS) * g


def _log_sigmoid(z):
    return jnp.minimum(z, 0.0) - jnp.log1p(jnp.exp(-jnp.abs(z)))


def _silu(x):
    return x * jax.nn.sigmoid(x)


def _dot(a, b, precision=None):
    return jnp.dot(a, b, preferred_element_type=F32, precision=precision)


def _dot_nt(a, b):
    return lax.dot_general(a, b, (((1,), (1,)), ((), ())), preferred_element_type=F32)


def _dot_tn(a, b):
    return lax.dot_general(a, b, (((0,), (0,)), ((), ())), preferred_element_type=F32)


def _resident(a):
    nd = a.ndim
    return pl.BlockSpec(a.shape, lambda *_: (0,) * nd, pipeline_mode=pl.Buffered(1))


def _row_call(body, row_ins, res_ins, outs, name, tm=ROW_TILE):
    rows = row_ins[0].shape[0]
    in_specs = [pl.BlockSpec((tm, a.shape[1]), lambda i: (i, 0)) for a in row_ins]
    in_specs += [_resident(a) for a in res_ins]
    out_specs = [pl.BlockSpec((tm, c), lambda i: (i, 0)) for c, _ in outs]
    out_shape = [jax.ShapeDtypeStruct((rows, c), dt) for c, dt in outs]
    return pl.pallas_call(body, grid=(rows // tm,), in_specs=in_specs, out_specs=out_specs,
                          out_shape=out_shape, compiler_params=_params(1), name=name)(*row_ins, *res_ins)


def _split_row_call(body, main, tail, row_ins, res_ins, outs, name):
    tm = TAIL_TILE
    n_main = R_MAIN // tm
    in_specs = [pl.BlockSpec((tm, main.shape[1]), lambda i: (jnp.minimum(i, n_main - 1), 0)),
                pl.BlockSpec((tm, tail.shape[1]), lambda i: (0, 0))]
    in_specs += [pl.BlockSpec((tm, a.shape[1]), lambda i: (i, 0)) for a in row_ins]
    in_specs += [_resident(a) for a in res_ins]
    out_specs = [pl.BlockSpec((tm, c), lambda i: (i, 0)) for c, _ in outs]
    out_shape = [jax.ShapeDtypeStruct((R_ALL, c), dt) for c, dt in outs]
    return pl.pallas_call(body, grid=(R_ALL // tm,), in_specs=in_specs, out_specs=out_specs,
                          out_shape=out_shape, compiler_params=_params(1), name=name)(main, tail, *row_ins, *res_ins)


def _pick_tile(main_ref, tail_ref):
    is_tail = pl.program_id(0) == R_MAIN // TAIL_TILE
    return jnp.where(is_tail, tail_ref[...], main_ref[...])


def _ffn_body(x_ref, g_ref, wg_ref, wu_ref, wd_ref, o_ref):
    x = x_ref[...]
    n = _rmsnorm(x, g_ref[...]).astype(BF16)
    acc = jnp.zeros(x.shape, F32)
    for c in range(D_FF // FF_CHUNK):
        lo, hi = c * FF_CHUNK, (c + 1) * FF_CHUNK
        gate = _dot(n, wg_ref[:, lo:hi])
        up = _dot(n, wu_ref[:, lo:hi])
        acc = acc + _dot((_silu(gate) * up).astype(BF16), wd_ref[lo:hi, :])
    o_ref[...] = x + 0.5 * acc


def _ffn(x, g, wg, wu, wd):
    return _row_call(_ffn_body, [x], [g, wg, wu, wd], [(D_MODEL, F32)], "ffn")[0]


def _gla_proj_body(x_ref, g_ref, wq_ref, wk_ref, wv_ref, wr_ref, wa_ref, w2_ref, ba_ref,
                   q_ref, k_ref, v_ref, r_ref, la_ref):
    n = _rmsnorm(x_ref[...], g_ref[...]).astype(BF16)
    q_ref[...] = _dot(n, wq_ref[...]) * (A_DK ** -0.5)
    k_ref[...] = _dot(n, wk_ref[...])
    v_ref[...] = _dot(n, wv_ref[...])
    r_ref[...] = _dot(n, wr_ref[...])
    a = _dot(n, wa_ref[...]).astype(BF16)
    z = _dot(a, w2_ref[...]) + ba_ref[...]
    la_ref[...] = _log_sigmoid(z) * (1.0 / A_GATE_TAU)


def _gla_proj(x, g, w_in, w_alpha2, b_alpha):
    w = w_in.astype(BF16)
    c0, c1, c2, c3 = A_DK_TOT, 2 * A_DK_TOT, 2 * A_DK_TOT + A_DV_TOT, 2 * A_DK_TOT + 2 * A_DV_TOT
    wa = jnp.pad(w[:, c3:], ((0, 0), (0, V7X_LANES - A_GATE_RANK)))
    w2 = jnp.pad(w_alpha2.astype(BF16), ((0, V7X_LANES - A_GATE_RANK), (0, 0)))
    res = [g, w[:, :c0], w[:, c0:c1], w[:, c1:c2], w[:, c2:c3], wa, w2, b_alpha.reshape(1, A_DK_TOT)]
    outs = [(A_DK_TOT, F32), (A_DK_TOT, F32), (A_DV_TOT, F32), (A_DV_TOT, F32), (A_DK_TOT, F32)]
    return _row_call(_gla_proj_body, [x], res, outs, "gla_proj")


def _gla_chunk(q, k, v, la, st, ecat_ref, n_sub):
    c = GLA_SUB * n_sub
    row = lax.broadcasted_iota(jnp.int32, (c, c), 0)
    col = lax.broadcasted_iota(jnp.int32, (c, c), 1)
    causal = row >= col
    b = _dot(causal.astype(F32), la, precision=HIGHEST)
    a_rows = []
    for i in range(n_sub):
        lo = GLA_SUB * i
        b_i, q_i, k_i = b[lo:lo + GLA_SUB], q[lo:lo + GLA_SUB], k[lo:lo + GLA_SUB]
        pieces = []
        for s in range(GLA_SUB):
            decay = jnp.exp(jnp.minimum(b_i - b_i[s:s + 1], 0.0))
            pieces.append((q_i * k_i[s:s + 1] * decay).astype(BF16))
        a_i = _dot(jnp.concatenate(pieces, axis=1), ecat_ref[i])
        if i > 0:
            r_i = b[lo - 1:lo]
            q_s = (q_i * jnp.exp(b_i - r_i)).astype(BF16)
            k_s = (k[:lo] * jnp.exp(r_i - b[:lo])).astype(BF16)
            k_s = jnp.concatenate([k_s, jnp.zeros((c - lo, A_DK), BF16)], axis=0)
            a_i = a_i + _dot_nt(q_s, k_s)
        a_rows.append(a_i)
    a = jnp.concatenate(a_rows, axis=0)[:, :c]
    a = jnp.where(causal, a, 0.0)
    vb = v.astype(BF16)
    o = _dot(a.astype(BF16), vb)
    b_end = b[c - 1:c]
    k_d = (k * jnp.exp(b_end - b)).astype(BF16)
    st_new = _dot_tn(vb, k_d)
    if st is not None:
        o = o + _dot_nt((q * jnp.exp(b)).astype(BF16), st.astype(BF16))
        st_new = st_new + st * jnp.exp(b_end)
    return o, st_new


def _gla_scan_body(q_ref, k_ref, v_ref, la_ref, qt_ref, kt_ref, vt_ref, lat_ref, ecat_ref,
                   o_ref, ot_ref, s_ref, st_ref):
    heads = [(slice(g * A_DK, (g + 1) * A_DK), slice(g * A_DV, (g + 1) * A_DV)) for g in range(GLA_HEADS_PER_STEP)]
    for g, (kl, vl) in enumerate(heads):
        o_meta, st = _gla_chunk(qt_ref[:, kl], kt_ref[:, kl], vt_ref[:, vl], lat_ref[:, kl], None, ecat_ref, 1)
        ot_ref[:, vl] = o_meta
        st_ref[g] = st

    def step(j, carry):
        rows = pl.ds(pl.multiple_of(j * GLA_CHUNK, GLA_CHUNK), GLA_CHUNK)
        for g, (kl, vl) in enumerate(heads):
            o, st_new = _gla_chunk(q_ref[rows, kl], k_ref[rows, kl], v_ref[rows, vl], la_ref[rows, kl],
                                   st_ref[g], ecat_ref, GLA_CHUNK // GLA_SUB)
            o_ref[rows, vl] = o
            st_ref[g] = st_new
        return carry

    lax.fori_loop(0, SEQ // GLA_CHUNK, step, 0)
    for g in range(GLA_HEADS_PER_STEP):
        s_ref[0, g] = st_ref[g].T


def _gla_ecat():
    n_sub = GLA_CHUNK // GLA_SUB
    key = lax.broadcasted_iota(jnp.int32, (n_sub, GLA_SUB * A_DK, GLA_CHUNK), 1) // A_DK
    sub = lax.broadcasted_iota(jnp.int32, (n_sub, GLA_SUB * A_DK, GLA_CHUNK), 0)
    col = lax.broadcasted_iota(jnp.int32, (n_sub, GLA_SUB * A_DK, GLA_CHUNK), 2)
    return (col == GLA_SUB * sub + key).astype(BF16)


def _gla_scan(q, k, v, la):
    meta0 = R_MAIN // N_META
    n = GLA_HEADS_PER_STEP
    main_k = pl.BlockSpec((SEQ, n * A_DK), lambda b, h: (b, h))
    main_v = pl.BlockSpec((SEQ, n * A_DV), lambda b, h: (b, h))
    meta_k = pl.BlockSpec((N_META, n * A_DK), lambda b, h: (meta0 + b, h))
    meta_v = pl.BlockSpec((N_META, n * A_DV), lambda b, h: (meta0 + b, h))
    ecat = _gla_ecat()
    return pl.pallas_call(
        _gla_scan_body, grid=(BATCH, A_HEADS // n),
        in_specs=[main_k, main_k, main_v, main_k, meta_k, meta_k, meta_v, meta_k, _resident(ecat)],
        out_specs=[pl.BlockSpec((SEQ, n * A_DV), lambda b, h: (b, h)),
                   pl.BlockSpec((N_META, n * A_DV), lambda b, h: (b, h)),
                   pl.BlockSpec((1, n, A_DK, A_DV), lambda b, h: (b, h, 0, 0))],
        out_shape=[jax.ShapeDtypeStruct((R_MAIN, A_DV_TOT), F32),
                   jax.ShapeDtypeStruct((R_META, A_DV_TOT), F32),
                   jax.ShapeDtypeStruct((BATCH, A_HEADS, A_DK, A_DV), F32)],
        scratch_shapes=[pltpu.VMEM((n, A_DV, A_DK), F32)],
        compiler_params=_params(2), name="gla_scan")(q, k, v, la, q, k, v, la, ecat)


def _gla_step_body(s0_ref, qt_ref, kt_ref, lat_ref, v_ref, s1_ref, o_ref):
    for bi in range(GLA_STEP_BB):
        for h in range(A_HEADS):
            cols = slice(h * A_DV, (h + 1) * A_DV)
            k_col = kt_ref[0, h, :, bi:bi + 1]
            q_col = qt_ref[0, h, :, bi:bi + 1]
            decay = jnp.exp(lat_ref[0, h, :, bi:bi + 1])
            s1 = decay * s0_ref[bi, h] + k_col * v_ref[bi:bi + 1, cols]
            s1_ref[bi, h] = s1
            o_ref[bi:bi + 1, cols] = jnp.sum(q_col * s1, axis=0, keepdims=True)


def _gla_step(s0, q, k, v, la):
    n_steps = DEC_BATCH // GLA_STEP_BB

    def cols(a):
        a = a[R_PROMPT:].reshape(n_steps, GLA_STEP_BB, A_HEADS, A_DK)
        return jnp.transpose(a, (0, 2, 3, 1))

    col_spec = pl.BlockSpec((1, A_HEADS, A_DK, GLA_STEP_BB), lambda i: (i, 0, 0, 0))
    st_spec = pl.BlockSpec((GLA_STEP_BB, A_HEADS, A_DK, A_DV), lambda i: (i, 0, 0, 0))
    v0 = R_PROMPT // GLA_STEP_BB
    return pl.pallas_call(
        _gla_step_body, grid=(n_steps,),
        in_specs=[st_spec, col_spec, col_spec, col_spec,
                  pl.BlockSpec((GLA_STEP_BB, A_DV_TOT), lambda i: (v0 + i, 0))],
        out_specs=[st_spec, pl.BlockSpec((GLA_STEP_BB, A_DV_TOT), lambda i: (i, 0))],
        out_shape=[jax.ShapeDtypeStruct(s0.shape, F32),
                   jax.ShapeDtypeStruct((R_SAMPLE, A_DV_TOT), F32)],
        compiler_params=_params(1), name="gla_step")(s0, cols(q), cols(k), cols(la), v)


def _gla_out_body(om_ref, ot_ref, r_ref, x_ref, gh_ref, w_ref, y_ref):
    o = _pick_tile(om_ref, ot_ref)
    heads = []
    for h in range(A_HEADS):
        heads.append(_rmsnorm(o[:, h * A_DV:(h + 1) * A_DV], gh_ref[...]))
    gated = jnp.concatenate(heads, axis=1) * _silu(r_ref[...])
    y_ref[...] = x_ref[...] + _dot(gated.astype(BF16), w_ref[...])


def _gla_out(o_main, o_tail, r, x, g_head, w_out):
    return _split_row_call(_gla_out_body, o_main, o_tail, [r, x], [g_head.reshape(1, A_DV), w_out.astype(BF16)],
                           [(D_MODEL, F32)], "gla_out")[0]


def _kv_proj_body(x_ref, g_ref, wk_ref, wv_ref, wf_ref, bf_ref, k_ref, v_ref, kb_ref, vb_ref, lf_ref):
    n = _rmsnorm(x_ref[...], g_ref[...]).astype(BF16)
    k = _dot(n, wk_ref[...])
    v = _dot(n, wv_ref[...])
    k_ref[...] = k
    v_ref[...] = v
    kb_ref[...] = k.astype(BF16)
    vb_ref[...] = v.astype(BF16)
    f = _dot(n, wf_ref[...])[:, :B_HEADS] + bf_ref[...]
    lf_ref[...] = _log_sigmoid(f)


def _kv_proj(x, g, w_in, b_f):
    w = w_in.astype(BF16)
    wf = jnp.pad(w[:, 2 * D_MODEL:], ((0, 0), (0, V7X_LANES - B_HEADS)))
    res = [g, w[:, :D_MODEL], w[:, D_MODEL:2 * D_MODEL], wf, b_f.reshape(1, B_HEADS)]
    outs = [(D_MODEL, F32), (D_MODEL, F32), (D_MODEL, BF16), (D_MODEL, BF16), (B_HEADS, F32)]
    return _row_call(_kv_proj_body, [x], res, outs, "kv_proj")


def _cumsum_body(lm_ref, lt_ref, cm_ref, ct_ref):
    def tri(n):
        return (lax.broadcasted_iota(jnp.int32, (n, n), 0) >=
                lax.broadcasted_iota(jnp.int32, (n, n), 1)).astype(F32)

    c_meta = _dot(tri(N_META), lt_ref[...], precision=HIGHEST)
    ct_ref[...] = c_meta
    tri_blk = tri(V7X_LANES)

    def step(j, carry):
        rows = pl.ds(pl.multiple_of(j * V7X_LANES, V7X_LANES), V7X_LANES)
        c = _dot(tri_blk, lm_ref[rows, :], precision=HIGHEST) + carry
        cm_ref[rows, :] = c
        return c[V7X_LANES - 1:V7X_LANES]

    lax.fori_loop(0, SEQ // V7X_LANES, step, c_meta[N_META - 1:N_META])


def _cumsum_prompt(lf):
    meta0 = R_MAIN // N_META
    return pl.pallas_call(
        _cumsum_body, grid=(BATCH,),
        in_specs=[pl.BlockSpec((SEQ, B_HEADS), lambda b: (b, 0)),
                  pl.BlockSpec((N_META, B_HEADS), lambda b: (meta0 + b, 0))],
        out_specs=[pl.BlockSpec((SEQ, B_HEADS), lambda b: (b, 0)),
                   pl.BlockSpec((N_META, B_HEADS), lambda b: (b, 0))],
        out_shape=[jax.ShapeDtypeStruct((R_MAIN, B_HEADS), F32),
                   jax.ShapeDtypeStruct((R_META, B_HEADS), F32)],
        compiler_params=_params(1), name="cumsum_logf")(lf, lf)


def _fox_q_body(x_ref, g_ref, w_ref, q_ref):
    n = _rmsnorm(x_ref[...], g_ref[...]).astype(BF16)
    q_ref[...] = (_dot(n, w_ref[...]) * (B_HD ** -0.5)).astype(BF16)


def _fox_q_sample(x, g, w_q):
    return _row_call(_fox_q_body, [x[R_PROMPT:]], [g, w_q.astype(BF16)], [(D_MODEL, BF16)], "fox_q_sample",
                     tm=R_SAMPLE)[0]


AUG_LANES = V7X_LANES
AUG_WIDTH = B_HEADS * AUG_LANES
AUG_K_BIAS = B_HD
AUG_Q_BIAS = B_HD + 3


def _split3(c):
    hi = c.astype(BF16)
    r1 = c - hi.astype(F32)
    mid = r1.astype(BF16)
    lo = (r1 - mid.astype(F32)).astype(BF16)
    return hi, mid, lo


def _aug_consts(bias_lane, ones_lane, sign):
    lane = lax.broadcasted_iota(jnp.int32, (3, B_HEADS, AUG_WIDTH), 2)
    head = lax.broadcasted_iota(jnp.int32, (3, B_HEADS, AUG_WIDTH), 1)
    part = lax.broadcasted_iota(jnp.int32, (3, B_HEADS, AUG_WIDTH), 0)
    place = jnp.where(lane == head * AUG_LANES + bias_lane + part, sign, 0.0).astype(BF16)
    in_group = lax.broadcasted_iota(jnp.int32, (1, AUG_WIDTH), 1) % AUG_LANES
    ones = jnp.where((in_group >= ones_lane) & (in_group < ones_lane + 3), 1.0, 0.0).astype(F32)
    return place, ones


def _add_bias_lanes(x, c, place_ref, ones_ref):
    hi, mid, lo = _split3(c)
    return x + _dot(hi, place_ref[0]) + _dot(mid, place_ref[1]) + _dot(lo, place_ref[2]) + ones_ref[...]


def _fox_qaug_body(x_ref, c_ref, g_ref, w_ref, place_ref, ones_ref, q_ref):
    n = _rmsnorm(x_ref[...], g_ref[...]).astype(BF16)
    q = _dot(n, w_ref[...]) * (B_HD ** -0.5)
    q_ref[...] = _add_bias_lanes(q, c_ref[...], place_ref, ones_ref).astype(BF16)


def _fox_qaug(x, c, g, w_q):
    w = jnp.pad(w_q.astype(BF16).reshape(D_MODEL, B_HEADS, B_HD), ((0, 0), (0, 0), (0, AUG_LANES - B_HD)))
    place, ones = _aug_consts(AUG_Q_BIAS, AUG_K_BIAS, 1.0)
    return _row_call(_fox_qaug_body, [x, c], [g, w.reshape(D_MODEL, AUG_WIDTH), place, ones],
                     [(AUG_WIDTH, BF16)], "fox_qaug")[0]


def _fox_kaug_body(k_ref, c_ref, spread_ref, place_ref, ones_ref, o_ref):
    k = _dot(k_ref[...], spread_ref[...])
    o_ref[...] = _add_bias_lanes(k, c_ref[...], place_ref, ones_ref).astype(BF16)


def _fox_kaug(kb, c):
    src = lax.broadcasted_iota(jnp.int32, (D_MODEL, AUG_WIDTH), 0)
    dst = lax.broadcasted_iota(jnp.int32, (D_MODEL, AUG_WIDTH), 1)
    spread = (dst == (src // B_HD) * AUG_LANES + src % B_HD).astype(BF16)
    place, ones = _aug_consts(AUG_K_BIAS, AUG_Q_BIAS, -1.0)
    return _row_call(_fox_kaug_body, [kb, c], [spread, place, ones], [(AUG_WIDTH, BF16)], "fox_kaug")[0]


def _fox_out_body(om_ref, ot_ref, x_ref, w_ref, y_ref):
    y_ref[...] = x_ref[...] + _dot(_pick_tile(om_ref, ot_ref), w_ref[...])


def _fox_out(o_main, o_tail, x, w_out):
    return _split_row_call(_fox_out_body, o_main, o_tail, [x], [w_out.astype(BF16)], [(D_MODEL, F32)],
                           "fox_out")[0]


def _fox_prompt_body(q_ref, qt_ref, k_ref, kt_ref, vT_ref, vTt_ref, vt_ref, o_ref, ot_ref,
                     m_ref, l_ref, a_ref, acc_ref, s_ref, p_ref):
    i = pl.program_id(1)
    visible = (lax.broadcasted_iota(jnp.int32, (FOX_TK, FOX_TQ), 0) <=
               lax.broadcasted_iota(jnp.int32, (FOX_TK, FOX_TQ), 1))

    m_ref[...] = jnp.full(m_ref.shape, -jnp.inf, F32)
    l_ref[...] = jnp.zeros(l_ref.shape, F32)
    acc_ref[...] = jnp.zeros(acc_ref.shape, F32)

    def attend(n_keys, keys_of, values_of, masked):
        for h in range(B_HEADS):
            grp = slice(h * AUG_LANES, (h + 1) * AUG_LANES)
            s_ref[h, :n_keys] = _dot_nt(keys_of(grp), q_ref[:, grp])
        for h in range(B_HEADS):
            for c in range(FOX_TQ // V7X_LANES):
                qs = slice(c * V7X_LANES, (c + 1) * V7X_LANES)
                s = s_ref[h, :n_keys, qs]
                if masked:
                    s = jnp.where(visible[:, qs], s, -jnp.inf)
                m_old = m_ref[h:h + 1, qs]
                m_new = jnp.maximum(m_old, jnp.max(s, axis=0, keepdims=True))
                alpha = jnp.exp(m_old - m_new)
                p = jnp.exp(s - m_new)
                p_ref[h, :n_keys, qs] = p.astype(BF16)
                m_ref[h:h + 1, qs] = m_new
                a_ref[h:h + 1, qs] = alpha
                l_ref[h:h + 1, qs] = alpha * l_ref[h:h + 1, qs] + jnp.sum(p, axis=0, keepdims=True)
        for h in range(B_HEADS):
            rows = slice(h * B_HD, (h + 1) * B_HD)
            acc_ref[rows, :] = a_ref[h:h + 1, :] * acc_ref[rows, :] + _dot(values_of(rows), p_ref[h, :n_keys])

    def key_block(j, masked):
        keys = pl.ds(pl.multiple_of(j * FOX_TK, FOX_TK), FOX_TK)
        attend(FOX_TK, lambda grp: k_ref[keys, grp], lambda rows: vT_ref[0, j, rows, :], masked)

    def full_block(j, carry):
        key_block(j, False)
        return carry

    attend(N_META, lambda grp: kt_ref[:, grp], lambda rows: vTt_ref[0, rows, :], False)
    lax.fori_loop(0, i, full_block, 0)
    key_block(i, True)

    for h in range(B_HEADS):
        rows = slice(h * B_HD, (h + 1) * B_HD)
        acc_ref[rows, :] = acc_ref[rows, :] / l_ref[h:h + 1, :]
    o_ref[...] = acc_ref[...].T.astype(o_ref.dtype)

    @pl.when(i == 0)
    def _():
        tri = (lax.broadcasted_iota(jnp.int32, (N_META, N_META), 0) >=
               lax.broadcasted_iota(jnp.int32, (N_META, N_META), 1))
        for h in range(B_HEADS):
            grp = slice(h * AUG_LANES, (h + 1) * AUG_LANES)
            hs = slice(h * B_HD, (h + 1) * B_HD)
            s = jnp.where(tri, _dot_nt(qt_ref[:, grp], kt_ref[:, grp]), -jnp.inf)
            p = jnp.exp(s - jnp.max(s, axis=1, keepdims=True))
            o = _dot(p.astype(BF16), vt_ref[:, hs]) / jnp.sum(p, axis=1, keepdims=True)
            ot_ref[:, hs] = o.astype(ot_ref.dtype)


def _fox_values_t(vb):
    vT = jnp.transpose(vb[:R_MAIN].reshape(BATCH, SEQ // FOX_TK, FOX_TK, D_MODEL), (0, 1, 3, 2))
    vTt = jnp.transpose(vb[R_MAIN:R_PROMPT].reshape(BATCH, N_META, D_MODEL), (0, 2, 1))
    return vT, vTt


def _fox_prompt(qaug, kaug, vb, vT, vTt):
    meta0 = R_MAIN // N_META
    n_qblk = SEQ // FOX_TQ
    n_kblk = SEQ // FOX_TK
    return pl.pallas_call(
        _fox_prompt_body, grid=(BATCH, n_qblk),
        in_specs=[pl.BlockSpec((FOX_TQ, AUG_WIDTH), lambda b, i: (b * n_qblk + i, 0)),
                  pl.BlockSpec((N_META, AUG_WIDTH), lambda b, i: (meta0 + b, 0)),
                  pl.BlockSpec((SEQ, AUG_WIDTH), lambda b, i: (b, 0)),
                  pl.BlockSpec((N_META, AUG_WIDTH), lambda b, i: (meta0 + b, 0)),
                  pl.BlockSpec((1, n_kblk, D_MODEL, FOX_TK), lambda b, i: (b, 0, 0, 0)),
                  pl.BlockSpec((1, D_MODEL, N_META), lambda b, i: (b, 0, 0)),
                  pl.BlockSpec((N_META, D_MODEL), lambda b, i: (meta0 + b, 0))],
        out_specs=[pl.BlockSpec((FOX_TQ, D_MODEL), lambda b, i: (b * n_qblk + i, 0)),
                   pl.BlockSpec((N_META, D_MODEL), lambda b, i: (b, 0))],
        out_shape=[jax.ShapeDtypeStruct((R_MAIN, D_MODEL), BF16),
                   jax.ShapeDtypeStruct((R_META, D_MODEL), BF16)],
        scratch_shapes=[pltpu.VMEM((B_HEADS, FOX_TQ), F32), pltpu.VMEM((B_HEADS, FOX_TQ), F32),
                        pltpu.VMEM((B_HEADS, FOX_TQ), F32), pltpu.VMEM((D_MODEL, FOX_TQ), F32),
                        pltpu.VMEM((B_HEADS, FOX_TK, FOX_TQ), F32),
                        pltpu.VMEM((B_HEADS, FOX_TK, FOX_TQ), BF16)],
        compiler_params=_params(2), name="fox_prompt")(qaug, qaug, kaug, kaug, vT, vTt, vb)


def _cache_pack_body(pt_ref, lfn_ref, *refs):
    n = PAGES_PER_STEP
    k_refs, v_refs, lf_refs = refs[:n], refs[n:2 * n], refs[2 * n:3 * n]
    kc_ref, vc_ref, bias_ref, suf_ref = refs[3 * n:]

    @pl.when(pl.program_id(1) == 0)
    def _():
        suf_ref[...] = lfn_ref[0]

    after = (lax.broadcasted_iota(jnp.int32, (PAGE_SIZE, PAGE_SIZE), 1) >
             lax.broadcasted_iota(jnp.int32, (PAGE_SIZE, PAGE_SIZE), 0)).astype(F32)
    suf = suf_ref[...]
    for j in reversed(range(n)):
        rows = slice(j * PAGE_SIZE, (j + 1) * PAGE_SIZE)
        for pair in range(B_HEADS // 2):
            lanes = slice(pair * 2 * B_HD, (pair + 1) * 2 * B_HD)
            for src, dst in ((k_refs[j], kc_ref), (v_refs[j], vc_ref)):
                flat = src.reshape(PAGE_SIZE * B_HEADS, B_HD)
                even = flat[pl.ds(2 * pair, PAGE_SIZE, stride=B_HEADS), :]
                odd = flat[pl.ds(2 * pair + 1, PAGE_SIZE, stride=B_HEADS), :]
                dst[0, rows, lanes] = jnp.concatenate([even, odd], axis=1).astype(BF16)
        lf = lf_refs[j][0]
        bias = _dot(after, lf, precision=HIGHEST) + suf
        suf = bias[0:1] + lf[0:1]
        bias_ref[0, rows, :] = bias
    suf_ref[...] = suf


def _cache_pack(page_table, lf, cache_k, cache_v, cache_logf):
    n = PAGES_PER_STEP
    n_groups = N_PAGES // n

    def group(t):
        return n_groups - 1 - t

    def page_spec(j, tail):
        return pl.BlockSpec((1, PAGE_SIZE) + tail,
                            lambda b, t, pt: (pt[b, group(t) * n + j],) + (0,) * (1 + len(tail)))

    def out_spec(width):
        return pl.BlockSpec((1, n * PAGE_SIZE, width), lambda b, t, pt: (b, group(t), 0))

    in_specs = [pl.BlockSpec((1, 1, B_HEADS), lambda b, t, pt: (b, 0, 0))]
    in_specs += [page_spec(j, (B_HEADS, B_HD)) for j in range(n)]
    in_specs += [page_spec(j, (B_HEADS, B_HD)) for j in range(n)]
    in_specs += [page_spec(j, (B_HEADS,)) for j in range(n)]
    grid_spec = pltpu.PrefetchScalarGridSpec(
        num_scalar_prefetch=1, grid=(DEC_BATCH, n_groups), in_specs=in_specs,
        out_specs=[out_spec(D_MODEL), out_spec(D_MODEL), out_spec(B_HEADS)],
        scratch_shapes=[pltpu.VMEM((1, B_HEADS), F32)])
    lf_new = lf[R_PROMPT:].reshape(DEC_BATCH, 1, B_HEADS)
    return pl.pallas_call(
        _cache_pack_body, grid_spec=grid_spec,
        out_shape=[jax.ShapeDtypeStruct((DEC_BATCH, PAST_LEN, D_MODEL), BF16),
                   jax.ShapeDtypeStruct((DEC_BATCH, PAST_LEN, D_MODEL), BF16),
                   jax.ShapeDtypeStruct((DEC_BATCH, PAST_LEN, B_HEADS), F32)],
        compiler_params=_params(2), name="cache_pack")(
            page_table, lf_new, *([cache_k] * n), *([cache_v] * n), *([cache_logf] * n))


def _fox_decode_body(q_ref, kn_ref, vn_ref, kc_ref, vc_ref, bias_ref, o_ref):
    lane_head = lax.broadcasted_iota(jnp.int32, (B_HEADS, D_MODEL), 1) // B_HD
    head = lax.broadcasted_iota(jnp.int32, (B_HEADS, D_MODEL), 0)
    own = jnp.where(lane_head == head, 1.0, 0.0)
    q_rows = q_ref[0].astype(F32) * own
    s = _dot_nt(q_rows.astype(BF16), kc_ref[0]) + bias_ref[0]
    k_new = kn_ref[0].astype(BF16).astype(F32)
    s_new = jnp.sum(q_rows * k_new, axis=1, keepdims=True)
    m = jnp.maximum(jnp.max(s, axis=1, keepdims=True), s_new)
    p = jnp.exp(s - m)
    p_new = jnp.exp(s_new - m)
    l = jnp.sum(p, axis=1, keepdims=True) + p_new
    out = _dot(p.astype(BF16), vc_ref[0]) + p_new * vn_ref[0]
    o_ref[0] = jnp.sum(out * own / l, axis=0, keepdims=True).astype(o_ref.dtype)


def _fox_decode(q_s, k, v, kc, vc, bias_t):
    def new_rows(a):
        return a[R_PROMPT:].reshape(DEC_BATCH, 1, D_MODEL)

    row = pl.BlockSpec((1, 1, D_MODEL), lambda b: (b, 0, 0))
    dense = pl.BlockSpec((1, PAST_LEN, D_MODEL), lambda b: (b, 0, 0))
    o = pl.pallas_call(
        _fox_decode_body, grid=(DEC_BATCH,),
        in_specs=[row, row, row, dense, dense, pl.BlockSpec((1, B_HEADS, PAST_LEN), lambda b: (b, 0, 0))],
        out_specs=row, out_shape=jax.ShapeDtypeStruct((DEC_BATCH, 1, D_MODEL), BF16),
        compiler_params=_params(1), name="fox_decode")(
            q_s.reshape(DEC_BATCH, 1, D_MODEL), new_rows(k), new_rows(v), kc, vc, bias_t)
    return o.reshape(DEC_BATCH, D_MODEL)


def _final_norm_body(x_ref, g_ref, y_ref):
    y_ref[...] = _rmsnorm(x_ref[...], g_ref[...])


def _final_norm(x, g):
    return _row_call(_final_norm_body, [x], [g], [(D_MODEL, F32)], "final_norm")[0]


def kernel(x_prompt, x_sample, state_gla, cache_k, cache_v, cache_logf, page_table, meta_tokens,
           norm_ffn1, ffn1_w_gate, ffn1_w_up, ffn1_w_down, norm_mix, norm_ffn2,
           ffn2_w_gate, ffn2_w_up, ffn2_w_down, gla_w_in, gla_w_alpha2, gla_b_alpha, gla_g_head,
           gla_w_out, kv_norm, kv_w_in, kv_b_f, fox_w_q, fox_w_out, final_norm):
    def gain(g):
        return g.reshape(1, D_MODEL)

    meta = jnp.broadcast_to(meta_tokens[None], (BATCH, N_META, D_MODEL)).reshape(R_META, D_MODEL)
    x = jnp.concatenate([x_prompt.reshape(R_MAIN, D_MODEL), meta, x_sample.reshape(R_SAMPLE, D_MODEL)], axis=0)

    states_p, states_s = [], []
    k = v = vb = vT = vTt = c_rows = kaug = kc = vc = bias_t = None
    for l in range(DEPTH):
        x = _ffn(x, gain(norm_ffn1[l]), ffn1_w_gate[l].astype(BF16), ffn1_w_up[l].astype(BF16),
                 ffn1_w_down[l].astype(BF16))
        if l < N_A_LAYERS:
            q, kk, vv, r, la = _gla_proj(x, gain(norm_mix[l]), gla_w_in[l], gla_w_alpha2[l], gla_b_alpha[l])
            o_main, o_meta, st_p = _gla_scan(q, kk, vv, la)
            st_s, o_s = _gla_step(state_gla[l], q, kk, vv, la)
            states_p.append(st_p)
            states_s.append(st_s)
            x = _gla_out(o_main, jnp.concatenate([o_meta, o_s], axis=0), r, x, gla_g_head[l], gla_w_out[l])
        else:
            lb = l - N_A_LAYERS
            qaug = _fox_qaug(x, c_rows, gain(norm_mix[l]), fox_w_q[lb])
            o_main, o_meta = _fox_prompt(qaug, kaug, vb, vT, vTt)
            q_s = _fox_q_sample(x, gain(norm_mix[l]), fox_w_q[lb])
            o_s = _fox_decode(q_s, k, v, kc, vc, bias_t)
            x = _fox_out(o_main, jnp.concatenate([o_meta, o_s], axis=0), x, fox_w_out[lb])
        x = _ffn(x, gain(norm_ffn2[l]), ffn2_w_gate[l].astype(BF16), ffn2_w_up[l].astype(BF16),
                 ffn2_w_down[l].astype(BF16))
        if l == N_A_LAYERS - 1:
            k, v, kb, vb, lf = _kv_proj(x, gain(kv_norm), kv_w_in, kv_b_f)
            c_main, c_meta = _cumsum_prompt(lf)
            c_rows = jnp.concatenate([c_main, c_meta, jnp.zeros((R_SAMPLE, B_HEADS), F32)], axis=0)
            kaug = _fox_kaug(kb, c_rows)
            vT, vTt = _fox_values_t(vb)
            kc, vc, bias = _cache_pack(page_table, lf, cache_k, cache_v, cache_logf)
            bias_t = jnp.transpose(bias, (0, 2, 1))
    y =_final_norm(x, gain(final_norm))

    def prompt_rows(a, width):
        return jnp.concatenate([a[R_MAIN:R_PROMPT].reshape(BATCH, N_META, width),
                                a[:R_MAIN].reshape(BATCH, SEQ, width)], axis=1)

    t_all = N_META + SEQ
    return (y[:R_MAIN].reshape(BATCH, SEQ, D_MODEL),
            y[R_PROMPT:].reshape(DEC_BATCH, 1, D_MODEL),
            jnp.stack(states_p, axis=0),
            prompt_rows(k, D_MODEL).reshape(BATCH, t_all, B_HEADS, B_HD),
            prompt_rows(v, D_MODEL).reshape(BATCH, t_all, B_HEADS, B_HD),
            prompt_rows(lf, B_HEADS),
            jnp.stack(states_s, axis=0),
            k[R_PROMPT:].reshape(DEC_BATCH, 1, B_HEADS, B_HD),
            v[R_PROMPT:].reshape(DEC_BATCH, 1, B_HEADS, B_HD),
            lf[R_PROMPT:].reshape(DEC_BATCH, 1, B_HEADS))
```

```python
import functools

import jax
import jax.numpy as jnp
from jax import lax
from jax.experimental import pallas as pl
from jax.experimental.pallas import tpu as pltpu

F32 = jnp.float32
BF16 = jnp.bfloat16
HIGHEST = lax.Precision.HIGHEST

D_MODEL = 1024
BATCH = 8
SEQ = 2048
DEPTH = 4
DEC_BATCH = 128
PAST_LEN = 2048
PAGE_SIZE = 128
N_PAGES = PAST_LEN // PAGE_SIZE
N_META = 16
N_A_LAYERS = 2
A_HEADS = 4
A_DK_TOT = 512
A_DV_TOT = 1024
A_DK = 128
A_DV = 256
A_GATE_RANK = 16
A_GATE_TAU = 16.0
B_HEADS = 16
B_HD = 64
D_FF = 2816
EPS = 1e-6

R_MAIN = BATCH * SEQ
R_META = BATCH * N_META
R_SAMPLE = DEC_BATCH
R_PROMPT = R_MAIN + R_META
R_ALL = R_PROMPT + R_SAMPLE

V7X_LANES = 128
V7X_VMEM_LIMIT = 56 * 1024 * 1024

ROW_TILE = 640
TAIL_TILE = R_META + R_SAMPLE
FF_CHUNK = 1408
GLA_CHUNK = 128
GLA_SUB = 16
GLA_HEADS_PER_STEP = 2
FOX_TQ = 256
FOX_TK = 256
GLA_STEP_BB = 8


def _params(n_grid):
    return pltpu.CompilerParams(dimension_semantics=("arbitrary",) * n_grid,
                                vmem_limit_bytes=V7X_VMEM_LIMIT)


def _rmsnorm(x, g):
    return x * lax.rsqrt(jnp.mean(x * x, axis=-1, keepdims=True) + EPS) * g


def _log_sigmoid(z):
    return jnp.minimum(z, 0.0) - jnp.log1p(jnp.exp(-jnp.abs(z)))


def _silu(x):
    return x * jax.nn.sigmoid(x)


def _dot(a, b, precision=None):
    return jnp.dot(a, b, preferred_element_type=F32, precision=precision)


def _dot_nt(a, b):
    return lax.dot_general(a, b, (((1,), (1,)), ((), ())), preferred_element_type=F32)


def _dot_tn(a, b):
    return lax.dot_general(a, b, (((0,), (0,)), ((), ())), preferred_element_type=F32)


def _resident(a):
    nd = a.ndim
    return pl.BlockSpec(a.shape, lambda *_: (0,) * nd, pipeline_mode=pl.Buffered(1))


def _row_call(body, row_ins, res_ins, outs, name, tm=ROW_TILE):
    rows = row_ins[0].shape[0]
    in_specs = [pl.BlockSpec((tm, a.shape[1]), lambda i: (i, 0)) for a in row_ins]
    in_specs += [_resident(a) for a in res_ins]
    out_specs = [pl.BlockSpec((tm, c), lambda i: (i, 0)) for c, _ in outs]
    out_shape = [jax.ShapeDtypeStruct((rows, c), dt) for c, dt in outs]
    return pl.pallas_call(body, grid=(rows // tm,), in_specs=in_specs, out_specs=out_specs,
                          out_shape=out_shape, compiler_params=_params(1), name=name)(*row_ins, *res_ins)


def _split_row_call(body, main, tail, row_ins, res_ins, outs, name):
    tm = TAIL_TILE
    n_main = R_MAIN // tm
    in_specs = [pl.BlockSpec((tm, main.shape[1]), lambda i: (jnp.minimum(i, n_main - 1), 0)),
                pl.BlockSpec((tm, tail.shape[1]), lambda i: (0, 0))]
    in_specs += [pl.BlockSpec((tm, a.shape[1]), lambda i: (i, 0)) for a in row_ins]
    in_specs += [_resident(a) for a in res_ins]
    out_specs = [pl.BlockSpec((tm, c), lambda i: (i, 0)) for c, _ in outs]
    out_shape = [jax.ShapeDtypeStruct((R_ALL, c), dt) for c, dt in outs]
    return pl.pallas_call(body, grid=(R_ALL // tm,), in_specs=in_specs, out_specs=out_specs,
                          out_shape=out_shape, compiler_params=_params(1), name=name)(main, tail, *row_ins, *res_ins)


def _pick_tile(main_ref, tail_ref):
    is_tail = pl.program_id(0) == R_MAIN // TAIL_TILE
    return jnp.where(is_tail, tail_ref[...], main_ref[...])


def _ffn_body(x_ref, g_ref, wg_ref, wu_ref, wd_ref, o_ref):
    x = x_ref[...]
    n = _rmsnorm(x, g_ref[...]).astype(BF16)
    acc = jnp.zeros(x.shape, F32)
    for c in range(D_FF // FF_CHUNK):
        lo, hi = c * FF_CHUNK, (c + 1) * FF_CHUNK
        gate = _dot(n, wg_ref[:, lo:hi])
        up = _dot(n, wu_ref[:, lo:hi])
        acc = acc + _dot((_silu(gate) * up).astype(BF16), wd_ref[lo:hi, :])
    o_ref[...] = x + 0.5 * acc


def _ffn(x, g, wg, wu, wd):
    return _row_call(_ffn_body, [x], [g, wg, wu, wd], [(D_MODEL, F32)], "ffn")[0]


def _gla_proj_body(x_ref, g_ref, wq_ref, wk_ref, wv_ref, wr_ref, wa_ref, w2_ref, ba_ref,
                   q_ref, k_ref, v_ref, r_ref, la_ref):
    n = _rmsnorm(x_ref[...], g_ref[...]).astype(BF16)
    q_ref[...] = _dot(n, wq_ref[...]) * (A_DK ** -0.5)
    k_ref[...] = _dot(n, wk_ref[...])
    v_ref[...] = _dot(n, wv_ref[...])
    r_ref[...] = _dot(n, wr_ref[...])
    a = _dot(n, wa_ref[...]).astype(BF16)
    z = _dot(a, w2_ref[...]) + ba_ref[...]
    la_ref[...] = _log_sigmoid(z) * (1.0 / A_GATE_TAU)


def _gla_proj(x, g, w_in, w_alpha2, b_alpha):
    w = w_in.astype(BF16)
    c0, c1, c2, c3 = A_DK_TOT, 2 * A_DK_TOT, 2 * A_DK_TOT + A_DV_TOT, 2 * A_DK_TOT + 2 * A_DV_TOT
    wa = jnp.pad(w[:, c3:], ((0, 0), (0, V7X_LANES - A_GATE_RANK)))
    w2 = jnp.pad(w_alpha2.astype(BF16), ((0, V7X_LANES - A_GATE_RANK), (0, 0)))
    res = [g, w[:, :c0], w[:, c0:c1], w[:, c1:c2], w[:, c2:c3], wa, w2, b_alpha.reshape(1, A_DK_TOT)]
    outs = [(A_DK_TOT, F32), (A_DK_TOT, F32), (A_DV_TOT, F32), (A_DV_TOT, F32), (A_DK_TOT, F32)]
    return _row_call(_gla_proj_body, [x], res, outs, "gla_proj")


def _gla_chunk(q, k, v, la, st, ecat_ref, n_sub):
    c = GLA_SUB * n_sub
    row = lax.broadcasted_iota(jnp.int32, (c, c), 0)
    col = lax.broadcasted_iota(jnp.int32, (c, c), 1)
    causal = row >= col
    b = _dot(causal.astype(F32), la, precision=HIGHEST)
    a_rows = []
    for i in range(n_sub):
        lo = GLA_SUB * i
        b_i, q_i, k_i = b[lo:lo + GLA_SUB], q[lo:lo + GLA_SUB], k[lo:lo + GLA_SUB]
        pieces = []
        for s in range(GLA_SUB):
            decay = jnp.exp(jnp.minimum(b_i - b_i[s:s + 1], 0.0))
            pieces.append((q_i * k_i[s:s + 1] * decay).astype(BF16))
        a_i = _dot(jnp.concatenate(pieces, axis=1), ecat_ref[i])
        if i > 0:
            r_i = b[lo - 1:lo]
            q_s = (q_i * jnp.exp(b_i - r_i)).astype(BF16)
            k_s = (k[:lo] * jnp.exp(r_i - b[:lo])).astype(BF16)
            k_s = jnp.concatenate([k_s, jnp.zeros((c - lo, A_DK), BF16)], axis=0)
            a_i = a_i + _dot_nt(q_s, k_s)
        a_rows.append(a_i)
    a = jnp.concatenate(a_rows, axis=0)[:, :c]
    a = jnp.where(causal, a, 0.0)
    vb = v.astype(BF16)
    o = _dot(a.astype(BF16), vb)
    b_end = b[c - 1:c]
    k_d = (k * jnp.exp(b_end - b)).astype(BF16)
    st_new = _dot_tn(vb, k_d)
    if st is not None:
        o = o + _dot_nt((q * jnp.exp(b)).astype(BF16), st.astype(BF16))
        st_new = st_new + st * jnp.exp(b_end)
    return o, st_new


def _gla_scan_body(q_ref, k_ref, v_ref, la_ref, qt_ref, kt_ref, vt_ref, lat_ref, ecat_ref,
                   o_ref, ot_ref, s_ref, st_ref):
    heads = [(slice(g * A_DK, (g + 1) * A_DK), slice(g * A_DV, (g + 1) * A_DV)) for g in range(GLA_HEADS_PER_STEP)]
    for g, (kl, vl) in enumerate(heads):
        o_meta, st = _gla_chunk(qt_ref[:, kl], kt_ref[:, kl], vt_ref[:, vl], lat_ref[:, kl], None, ecat_ref, 1)
        ot_ref[:, vl] = o_meta
        st_ref[g] = st

    def step(j, carry):
        rows = pl.ds(pl.multiple_of(j * GLA_CHUNK, GLA_CHUNK), GLA_CHUNK)
        for g, (kl, vl) in enumerate(heads):
            o, st_new = _gla_chunk(q_ref[rows, kl], k_ref[rows, kl], v_ref[rows, vl], la_ref[rows, kl],
                                   st_ref[g], ecat_ref, GLA_CHUNK // GLA_SUB)
            o_ref[rows, vl] = o
            st_ref[g] = st_new
        return carry

    lax.fori_loop(0, SEQ // GLA_CHUNK, step, 0)
    for g in range(GLA_HEADS_PER_STEP):
        s_ref[0, g] = st_ref[g].T


def _gla_ecat():
    n_sub = GLA_CHUNK // GLA_SUB
    key = lax.broadcasted_iota(jnp.int32, (n_sub, GLA_SUB * A_DK, GLA_CHUNK), 1) // A_DK
    sub = lax.broadcasted_iota(jnp.int32, (n_sub, GLA_SUB * A_DK, GLA_CHUNK), 0)
    col = lax.broadcasted_iota(jnp.int32, (n_sub, GLA_SUB * A_DK, GLA_CHUNK), 2)
    return (col == GLA_SUB * sub + key).astype(BF16)


def _gla_scan(q, k, v, la):
    meta0 = R_MAIN // N_META
    n = GLA_HEADS_PER_STEP
    main_k = pl.BlockSpec((SEQ, n * A_DK), lambda b, h: (b, h))
    main_v = pl.BlockSpec((SEQ, n * A_DV), lambda b, h: (b, h))
    meta_k = pl.BlockSpec((N_META, n * A_DK), lambda b, h: (meta0 + b, h))
    meta_v = pl.BlockSpec((N_META, n * A_DV), lambda b, h: (meta0 + b, h))
    ecat = _gla_ecat()
    return pl.pallas_call(
        _gla_scan_body, grid=(BATCH, A_HEADS // n),
        in_specs=[main_k, main_k, main_v, main_k, meta_k, meta_k, meta_v, meta_k, _resident(ecat)],
        out_specs=[pl.BlockSpec((SEQ, n * A_DV), lambda b, h: (b, h)),
                   pl.BlockSpec((N_META, n * A_DV), lambda b, h: (b, h)),
                   pl.BlockSpec((1, n, A_DK, A_DV), lambda b, h: (b, h, 0, 0))],
        out_shape=[jax.ShapeDtypeStruct((R_MAIN, A_DV_TOT), F32),
                   jax.ShapeDtypeStruct((R_META, A_DV_TOT), F32),
                   jax.ShapeDtypeStruct((BATCH, A_HEADS, A_DK, A_DV), F32)],
        scratch_shapes=[pltpu.VMEM((n, A_DV, A_DK), F32)],
        compiler_params=_params(2), name="gla_scan")(q, k, v, la, q, k, v, la, ecat)


def _gla_step_body(s0_ref, qt_ref, kt_ref, lat_ref, v_ref, s1_ref, o_ref):
    for bi in range(GLA_STEP_BB):
        for h in range(A_HEADS):
            cols = slice(h * A_DV, (h + 1) * A_DV)
            k_col = kt_ref[0, h, :, bi:bi + 1]
            q_col = qt_ref[0, h, :, bi:bi + 1]
            decay = jnp.exp(lat_ref[0, h, :, bi:bi + 1])
            s1 = decay * s0_ref[bi, h] + k_col * v_ref[bi:bi + 1, cols]
            s1_ref[bi, h] = s1
            o_ref[bi:bi + 1, cols] = jnp.sum(q_col * s1, axis=0, keepdims=True)


def _gla_step(s0, q, k, v, la):
    n_steps = DEC_BATCH // GLA_STEP_BB

    def cols(a):
        a = a[R_PROMPT:].reshape(n_steps, GLA_STEP_BB, A_HEADS, A_DK)
        return jnp.transpose(a, (0, 2, 3, 1))

    col_spec = pl.BlockSpec((1, A_HEADS, A_DK, GLA_STEP_BB), lambda i: (i, 0, 0, 0))
    st_spec = pl.BlockSpec((GLA_STEP_BB, A_HEADS, A_DK, A_DV), lambda i: (i, 0, 0, 0))
    v0 = R_PROMPT // GLA_STEP_BB
    return pl.pallas_call(
        _gla_step_body, grid=(n_steps,),
        in_specs=[st_spec, col_spec, col_spec, col_spec,
                  pl.BlockSpec((GLA_STEP_BB, A_DV_TOT), lambda i: (v0 + i, 0))],
        out_specs=[st_spec, pl.BlockSpec((GLA_STEP_BB, A_DV_TOT), lambda i: (i, 0))],
        out_shape=[jax.ShapeDtypeStruct(s0.shape, F32),
                   jax.ShapeDtypeStruct((R_SAMPLE, A_DV_TOT), F32)],
        compiler_params=_params(1), name="gla_step")(s0, cols(q), cols(k), cols(la), v)


def _gla_out_body(om_ref, ot_ref, r_ref, x_ref, gh_ref, w_ref, y_ref):
    o = _pick_tile(om_ref, ot_ref)
    heads = []
    for h in range(A_HEADS):
        heads.append(_rmsnorm(o[:, h * A_DV:(h + 1) * A_DV], gh_ref[...]))
    gated = jnp.concatenate(heads, axis=1) * _silu(r_ref[...])
    y_ref[...] = x_ref[...] + _dot(gated.astype(BF16), w_ref[...])


def _gla_out(o_main, o_tail, r, x, g_head, w_out):
    return _split_row_call(_gla_out_body, o_main, o_tail, [r, x], [g_head.reshape(1, A_DV), w_out.astype(BF16)],
                           [(D_MODEL, F32)], "gla_out")[0]


def _kv_proj_body(x_ref, g_ref, wk_ref, wv_ref, wf_ref, bf_ref, k_ref, v_ref, kb_ref, vb_ref, lf_ref):
    n = _rmsnorm(x_ref[...], g_ref[...]).astype(BF16)
    k = _dot(n, wk_ref[...])
    v = _dot(n, wv_ref[...])
    k_ref[...] = k
    v_ref[...] = v
    kb_ref[...] = k.astype(BF16)
    vb_ref[...] = v.astype(BF16)
    f = _dot(n, wf_ref[...])[:, :B_HEADS] + bf_ref[...]
    lf_ref[...] = _log_sigmoid(f)


def _kv_proj(x, g, w_in, b_f):
    w = w_in.astype(BF16)
    wf = jnp.pad(w[:, 2 * D_MODEL:], ((0, 0), (0, V7X_LANES - B_HEADS)))
    res = [g, w[:, :D_MODEL], w[:, D_MODEL:2 * D_MODEL], wf, b_f.reshape(1, B_HEADS)]
    outs = [(D_MODEL, F32), (D_MODEL, F32), (D_MODEL, BF16), (D_MODEL, BF16), (B_HEADS, F32)]
    return _row_call(_kv_proj_body, [x], res, outs, "kv_proj")


def _cumsum_body(lm_ref, lt_ref, cm_ref, ct_ref):
    def tri(n):
        return (lax.broadcasted_iota(jnp.int32, (n, n), 0) >=
                lax.broadcasted_iota(jnp.int32, (n, n), 1)).astype(F32)

    c_meta = _dot(tri(N_META), lt_ref[...], precision=HIGHEST)
    ct_ref[...] = c_meta
    tri_blk = tri(V7X_LANES)

    def step(j, carry):
        rows = pl.ds(pl.multiple_of(j * V7X_LANES, V7X_LANES), V7X_LANES)
        c = _dot(tri_blk, lm_ref[rows, :], precision=HIGHEST) + carry
        cm_ref[rows, :] = c
        return c[V7X_LANES - 1:V7X_LANES]

    lax.fori_loop(0, SEQ // V7X_LANES, step, c_meta[N_META - 1:N_META])


def _cumsum_prompt(lf):
    meta0 = R_MAIN // N_META
    return pl.pallas_call(
        _cumsum_body, grid=(BATCH,),
        in_specs=[pl.BlockSpec((SEQ, B_HEADS), lambda b: (b, 0)),
                  pl.BlockSpec((N_META, B_HEADS), lambda b: (meta0 + b, 0))],
        out_specs=[pl.BlockSpec((SEQ, B_HEADS), lambda b: (b, 0)),
                   pl.BlockSpec((N_META, B_HEADS), lambda b: (b, 0))],
        out_shape=[jax.ShapeDtypeStruct((R_MAIN, B_HEADS), F32),
                   jax.ShapeDtypeStruct((R_META, B_HEADS), F32)],
        compiler_params=_params(1), name="cumsum_logf")(lf, lf)


def _fox_q_body(x_ref, g_ref, w_ref, q_ref):
    n = _rmsnorm(x_ref[...], g_ref[...]).astype(BF16)
    q_ref[...] = (_dot(n, w_ref[...]) * (B_HD ** -0.5)).astype(BF16)


def _fox_q_sample(x, g, w_q):
    return _row_call(_fox_q_body, [x[R_PROMPT:]], [g, w_q.astype(BF16)], [(D_MODEL, BF16)], "fox_q_sample",
                     tm=R_SAMPLE)[0]


AUG_LANES = V7X_LANES
AUG_WIDTH = B_HEADS * AUG_LANES
AUG_K_BIAS = B_HD
AUG_Q_BIAS = B_HD + 3


def _split3(c):
    hi = c.astype(BF16)
    r1 = c - hi.astype(F32)
    mid = r1.astype(BF16)
    lo = (r1 - mid.astype(F32)).astype(BF16)
    return hi, mid, lo


def _aug_consts(bias_lane, ones_lane, sign):
    lane = lax.broadcasted_iota(jnp.int32, (3, B_HEADS, AUG_WIDTH), 2)
    head = lax.broadcasted_iota(jnp.int32, (3, B_HEADS, AUG_WIDTH), 1)
    part = lax.broadcasted_iota(jnp.int32, (3, B_HEADS, AUG_WIDTH), 0)
    place = jnp.where(lane == head * AUG_LANES + bias_lane + part, sign, 0.0).astype(BF16)
    in_group = lax.broadcasted_iota(jnp.int32, (1, AUG_WIDTH), 1) % AUG_LANES
    ones = jnp.where((in_group >= ones_lane) & (in_group < ones_lane + 3), 1.0, 0.0).astype(F32)
    return place, ones


def _add_bias_lanes(x, c, place_ref, ones_ref):
    hi, mid, lo = _split3(c)
    return x + _dot(hi, place_ref[0]) + _dot(mid, place_ref[1]) + _dot(lo, place_ref[2]) + ones_ref[...]


def _fox_qaug_body(x_ref, c_ref, g_ref, w_ref, place_ref, ones_ref, q_ref):
    n = _rmsnorm(x_ref[...], g_ref[...]).astype(BF16)
    q = _dot(n, w_ref[...]) * (B_HD ** -0.5)
    q_ref[...] = _add_bias_lanes(q, c_ref[...], place_ref, ones_ref).astype(BF16)


def _fox_qaug(x, c, g, w_q):
    w = jnp.pad(w_q.astype(BF16).reshape(D_MODEL, B_HEADS, B_HD), ((0, 0), (0, 0), (0, AUG_LANES - B_HD)))
    place, ones = _aug_consts(AUG_Q_BIAS, AUG_K_BIAS, 1.0)
    return _row_call(_fox_qaug_body, [x, c], [g, w.reshape(D_MODEL, AUG_WIDTH), place, ones],
                     [(AUG_WIDTH, BF16)], "fox_qaug")[0]


def _fox_kaug_body(k_ref, c_ref, spread_ref, place_ref, ones_ref, o_ref):
    k = _dot(k_ref[...], spread_ref[...])
    o_ref[...] = _add_bias_lanes(k, c_ref[...], place_ref, ones_ref).astype(BF16)


def _fox_kaug(kb, c):
    src = lax.broadcasted_iota(jnp.int32, (D_MODEL, AUG_WIDTH), 0)
    dst = lax.broadcasted_iota(jnp.int32, (D_MODEL, AUG_WIDTH), 1)
    spread = (dst == (src // B_HD) * AUG_LANES + src % B_HD).astype(BF16)
    place, ones = _aug_consts(AUG_K_BIAS, AUG_Q_BIAS, -1.0)
    return _row_call(_fox_kaug_body, [kb, c], [spread, place, ones], [(AUG_WIDTH, BF16)], "fox_kaug")[0]


def _fox_out_body(om_ref, ot_ref, x_ref, w_ref, y_ref):
    y_ref[...] = x_ref[...] + _dot(_pick_tile(om_ref, ot_ref), w_ref[...])


def _fox_out(o_main, o_tail, x, w_out):
    return _split_row_call(_fox_out_body, o_main, o_tail, [x], [w_out.astype(BF16)], [(D_MODEL, F32)],
                           "fox_out")[0]


def _fox_prompt_body(q_ref, qt_ref, k_ref, kt_ref, vT_ref, vTt_ref, vt_ref, o_ref, ot_ref,
                     m_ref, l_ref, a_ref, acc_ref, s_ref, p_ref):
    i = pl.program_id(1)
    visible = (lax.broadcasted_iota(jnp.int32, (FOX_TK, FOX_TQ), 0) <=
               lax.broadcasted_iota(jnp.int32, (FOX_TK, FOX_TQ), 1))

    m_ref[...] = jnp.full(m_ref.shape, -jnp.inf, F32)
    l_ref[...] = jnp.zeros(l_ref.shape, F32)
    acc_ref[...] = jnp.zeros(acc_ref.shape, F32)

    def attend(n_keys, keys_of, values_of, masked):
        for h in range(B_HEADS):
            grp = slice(h * AUG_LANES, (h + 1) * AUG_LANES)
            s_ref[h, :n_keys] = _dot_nt(keys_of(grp), q_ref[:, grp])
        for h in range(B_HEADS):
            for c in range(FOX_TQ // V7X_LANES):
                qs = slice(c * V7X_LANES, (c + 1) * V7X_LANES)
                s = s_ref[h, :n_keys, qs]
                if masked:
                    s = jnp.where(visible[:, qs], s, -jnp.inf)
                m_old = m_ref[h:h + 1, qs]
                m_new = jnp.maximum(m_old, jnp.max(s, axis=0, keepdims=True))
                alpha = jnp.exp(m_old - m_new)
                p = jnp.exp(s - m_new)
                p_ref[h, :n_keys, qs] = p.astype(BF16)
                m_ref[h:h + 1, qs] = m_new
                a_ref[h:h + 1, qs] = alpha
                l_ref[h:h + 1, qs] = alpha * l_ref[h:h + 1, qs] + jnp.sum(p, axis=0, keepdims=True)
        for h in range(B_HEADS):
            rows = slice(h * B_HD, (h + 1) * B_HD)
            acc_ref[rows, :] = a_ref[h:h + 1, :] * acc_ref[rows, :] + _dot(values_of(rows), p_ref[h, :n_keys])

    def key_block(j, masked):
        keys = pl.ds(pl.multiple_of(j * FOX_TK, FOX_TK), FOX_TK)
        attend(FOX_TK, lambda grp: k_ref[keys, grp], lambda rows: vT_ref[0, j, rows, :], masked)

    def full_block(j, carry):
        key_block(j, False)
        return carry

    attend(N_META, lambda grp: kt_ref[:, grp], lambda rows: vTt_ref[0, rows, :], False)
    lax.fori_loop(0, i, full_block, 0)
    key_block(i, True)

    for h in range(B_HEADS):
        rows = slice(h * B_HD, (h + 1) * B_HD)
        acc_ref[rows, :] = acc_ref[rows, :] / l_ref[h:h + 1, :]
    o_ref[...] = acc_ref[...].T.astype(o_ref.dtype)

    @pl.when(i == 0)
    def _():
        tri = (lax.broadcasted_iota(jnp.int32, (N_META, N_META), 0) >=
               lax.broadcasted_iota(jnp.int32, (N_META, N_META), 1))
        for h in range(B_HEADS):
            grp = slice(h * AUG_LANES, (h + 1) * AUG_LANES)
            hs = slice(h * B_HD, (h + 1) * B_HD)
            s = jnp.where(tri, _dot_nt(qt_ref[:, grp], kt_ref[:, grp]), -jnp.inf)
            p = jnp.exp(s - jnp.max(s, axis=1, keepdims=True))
            o = _dot(p.astype(BF16), vt_ref[:, hs]) / jnp.sum(p, axis=1, keepdims=True)
            ot_ref[:, hs] = o.astype(ot_ref.dtype)


def _fox_values_t(vb):
    vT = jnp.transpose(vb[:R_MAIN].reshape(BATCH, SEQ // FOX_TK, FOX_TK, D_MODEL), (0, 1, 3, 2))
    vTt = jnp.transpose(vb[R_MAIN:R_PROMPT].reshape(BATCH, N_META, D_MODEL), (0, 2, 1))
    return vT, vTt


def _fox_prompt(qaug, kaug, vb, vT, vTt):
    meta0 = R_MAIN // N_META
    n_qblk = SEQ // FOX_TQ
    n_kblk = SEQ // FOX_TK
    return pl.pallas_call(
        _fox_prompt_body, grid=(BATCH, n_qblk),
        in_specs=[pl.BlockSpec((FOX_TQ, AUG_WIDTH), lambda b, i: (b * n_qblk + i, 0)),
                  pl.BlockSpec((N_META, AUG_WIDTH), lambda b, i: (meta0 + b, 0)),
                  pl.BlockSpec((SEQ, AUG_WIDTH), lambda b, i: (b, 0)),
                  pl.BlockSpec((N_META, AUG_WIDTH), lambda b, i: (meta0 + b, 0)),
                  pl.BlockSpec((1, n_kblk, D_MODEL, FOX_TK), lambda b, i: (b, 0, 0, 0)),
                  pl.BlockSpec((1, D_MODEL, N_META), lambda b, i: (b, 0, 0)),
                  pl.BlockSpec((N_META, D_MODEL), lambda b, i: (meta0 + b, 0))],
        out_specs=[pl.BlockSpec((FOX_TQ, D_MODEL), lambda b, i: (b * n_qblk + i, 0)),
                   pl.BlockSpec((N_META, D_MODEL), lambda b, i: (b, 0))],
        out_shape=[jax.ShapeDtypeStruct((R_MAIN, D_MODEL), BF16),
                   jax.ShapeDtypeStruct((R_META, D_MODEL), BF16)],
        scratch_shapes=[pltpu.VMEM((B_HEADS, FOX_TQ), F32), pltpu.VMEM((B_HEADS, FOX_TQ), F32),
                        pltpu.VMEM((B_HEADS, FOX_TQ), F32), pltpu.VMEM((D_MODEL, FOX_TQ), F32),
                        pltpu.VMEM((B_HEADS, FOX_TK, FOX_TQ), F32),
                        pltpu.VMEM((B_HEADS, FOX_TK, FOX_TQ), BF16)],
        compiler_params=_params(2), name="fox_prompt")(qaug, qaug, kaug, kaug, vT, vTt, vb)


def _fox_decode_body(pt_ref, q_ref, qc_ref, kn_ref, vn_ref, lfn_ref, *refs):
    n = N_PAGES
    k_refs, v_refs, lf_refs = refs[:n], refs[n:2 * n], refs[2 * n:3 * n]
    o_ref, s_ref, acc_ref = refs[3 * n:]

    later = (lax.broadcasted_iota(jnp.int32, (PAGE_SIZE, PAGE_SIZE), 0) >
             lax.broadcasted_iota(jnp.int32, (PAGE_SIZE, PAGE_SIZE), 1)).astype(F32)
    suf = lfn_ref[0]
    for j in reversed(range(n)):
        lf = lf_refs[j][0]
        bias = _dot(lf, later, precision=HIGHEST) + suf
        suf = bias[:, 0:1] + lf[:, 0:1]
        s_ref[:, j * PAGE_SIZE:(j + 1) * PAGE_SIZE] = bias

    for h in range(B_HEADS):
        q_col = jnp.broadcast_to(qc_ref[0, h], (B_HD, PAGE_SIZE))
        for j in range(n):
            lanes = slice(j * PAGE_SIZE, (j + 1) * PAGE_SIZE)
            qk = jnp.sum(q_col * k_refs[j][0, h], axis=0, keepdims=True)
            s_ref[h:h + 1, lanes] = s_ref[h:h + 1, lanes] + qk

    lane_head = lax.broadcasted_iota(jnp.int32, (B_HEADS, D_MODEL), 1) // B_HD
    head = lax.broadcasted_iota(jnp.int32, (B_HEADS, D_MODEL), 0)
    own = jnp.where(lane_head == head, 1.0, 0.0)
    k_new = kn_ref[0].astype(BF16).astype(F32)
    s_new = jnp.sum(q_ref[0].astype(F32) * own * k_new, axis=1, keepdims=True)
    s = s_ref[...]
    m = jnp.maximum(jnp.max(s, axis=1, keepdims=True), s_new)
    p = jnp.exp(s - m)
    p_new = jnp.exp(s_new - m)
    l = jnp.sum(p, axis=1, keepdims=True) + p_new
    s_ref[...] = p

    for h in range(B_HEADS):
        acc = jnp.zeros((B_HD, PAGE_SIZE), F32)
        for j in range(n):
            acc = acc + s_ref[h:h + 1, j * PAGE_SIZE:(j + 1) * PAGE_SIZE] * v_refs[j][0, h]
        acc_ref[h * B_HD:(h + 1) * B_HD, :] = acc

    lane_sum = lax.dot_general(jnp.ones((8, PAGE_SIZE), F32), acc_ref[...], (((1,), (1,)), ((), ())),
                               preferred_element_type=F32, precision=HIGHEST)[0:1]
    new = jnp.sum(p_new * own, axis=0, keepdims=True) * vn_ref[0]
    o_ref[0] = ((lane_sum + new) / jnp.sum(l * own, axis=0, keepdims=True)).astype(o_ref.dtype)


def _cache_views(cache_k, cache_v, cache_logf):
    return (jnp.transpose(cache_k, (0, 2, 3, 1)), jnp.transpose(cache_v, (0, 2, 3, 1)),
            jnp.transpose(cache_logf, (0, 2, 1)))


def _fox_decode(page_table, q_s, k, v, lf, ck_t, cv_t, clf_t):
    n = N_PAGES

    def new_rows(a):
        return a[R_PROMPT:].reshape(DEC_BATCH, 1, D_MODEL)

    def page_spec(j, tail):
        return pl.BlockSpec((1, B_HEADS) + tail,
                            lambda b, pt: (pt[b, j],) + (0,) * (1 + len(tail)))

    row = pl.BlockSpec((1, 1, D_MODEL), lambda b, pt: (b, 0, 0))
    in_specs = [row, pl.BlockSpec((1, B_HEADS, B_HD, 1), lambda b, pt: (b, 0, 0, 0)), row, row,
                pl.BlockSpec((1, B_HEADS, 1), lambda b, pt: (b, 0, 0))]
    in_specs += [page_spec(j, (B_HD, PAGE_SIZE)) for j in range(n)]
    in_specs += [page_spec(j, (B_HD, PAGE_SIZE)) for j in range(n)]
    in_specs += [page_spec(j, (PAGE_SIZE,)) for j in range(n)]
    grid_spec = pltpu.PrefetchScalarGridSpec(
        num_scalar_prefetch=1, grid=(DEC_BATCH,), in_specs=in_specs, out_specs=row,
        scratch_shapes=[pltpu.VMEM((B_HEADS, PAST_LEN), F32), pltpu.VMEM((D_MODEL, PAGE_SIZE), F32)])
    q_cols = q_s.astype(F32).reshape(DEC_BATCH, B_HEADS, B_HD, 1)
    lf_new = lf[R_PROMPT:].reshape(DEC_BATCH, B_HEADS, 1)
    o = pl.pallas_call(
        _fox_decode_body, grid_spec=grid_spec,
        out_shape=jax.ShapeDtypeStruct((DEC_BATCH, 1, D_MODEL), BF16),
        compiler_params=_params(1), name="fox_decode")(
            page_table, q_s.reshape(DEC_BATCH, 1, D_MODEL), q_cols, new_rows(k), new_rows(v), lf_new,
            *([ck_t] * n), *([cv_t] * n), *([clf_t] * n))
    return o.reshape(DEC_BATCH, D_MODEL)


def _final_norm_body(x_ref, g_ref, y_ref):
    y_ref[...] = _rmsnorm(x_ref[...], g_ref[...])


def _final_norm(x, g):
    return _row_call(_final_norm_body, [x], [g], [(D_MODEL, F32)], "final_norm")[0]


def kernel(x_prompt, x_sample, state_gla, cache_k, cache_v, cache_logf, page_table, meta_tokens,
           norm_ffn1, ffn1_w_gate, ffn1_w_up, ffn1_w_down, norm_mix, norm_ffn2,
           ffn2_w_gate, ffn2_w_up, ffn2_w_down, gla_w_in, gla_w_alpha2, gla_b_alpha, gla_g_head,
           gla_w_out, kv_norm, kv_w_in, kv_b_f, fox_w_q, fox_w_out, final_norm):
    def gain(g):
        return g.reshape(1, D_MODEL)

    meta = jnp.broadcast_to(meta_tokens[None], (BATCH, N_META, D_MODEL)).reshape(R_META, D_MODEL)
    x = jnp.concatenate([x_prompt.reshape(R_MAIN, D_MODEL), meta, x_sample.reshape(R_SAMPLE, D_MODEL)], axis=0)

    states_p, states_s = [], []
    k = v = vb = vT = vTt = c_rows = kaug = None
    ck_t, cv_t, clf_t = _cache_views(cache_k, cache_v, cache_logf)
    for l in range(DEPTH):
        x = _ffn(x, gain(norm_ffn1[l]), ffn1_w_gate[l].astype(BF16), ffn1_w_up[l].astype(BF16),
                 ffn1_w_down[l].astype(BF16))
        if l < N_A_LAYERS:
            q, kk, vv, r, la = _gla_proj(x, gain(norm_mix[l]), gla_w_in[l], gla_w_alpha2[l], gla_b_alpha[l])
            o_main, o_meta, st_p = _gla_scan(q, kk, vv, la)
            st_s, o_s = _gla_step(state_gla[l], q, kk, vv, la)
            states_p.append(st_p)
            states_s.append(st_s)
            x = _gla_out(o_main, jnp.concatenate([o_meta, o_s], axis=0), r, x, gla_g_head[l], gla_w_out[l])
        else:
            lb = l - N_A_LAYERS
            qaug = _fox_qaug(x, c_rows, gain(norm_mix[l]), fox_w_q[lb])
            o_main, o_meta = _fox_prompt(qaug, kaug, vb, vT, vTt)
            q_s = _fox_q_sample(x, gain(norm_mix[l]), fox_w_q[lb])
            o_s = _fox_decode(page_table, q_s, k, v, lf, ck_t, cv_t, clf_t)
            x = _fox_out(o_main, jnp.concatenate([o_meta, o_s], axis=0), x, fox_w_out[lb])
        x = _ffn(x, gain(norm_ffn2[l]), ffn2_w_gate[l].astype(BF16), ffn2_w_up[l].astype(BF16),
                 ffn2_w_down[l].astype(BF16))
        if l == N_A_LAYERS - 1:
            k, v, kb, vb, lf = _kv_proj(x, gain(kv_norm), kv_w_in, kv_b_f)
            c_main, c_meta = _cumsum_prompt(lf)
            c_rows = jnp.concatenate([c_main, c_meta, jnp.zeros((R_SAMPLE, B_HEADS), F32)], axis=0)
            kaug = _fox_kaug(kb, c_rows)
            vT, vTt = _fox_values_t(vb)
    y =_final_norm(x, gain(final_norm))

    def prompt_rows(a, width):
        return jnp.concatenate([a[R_MAIN:R_PROMPT].reshape(BATCH, N_META, width),
                                a[:R_MAIN].reshape(BATCH, SEQ, width)], axis=1)

    t_all = N_META + SEQ
    return (y[:R_MAIN].reshape(BATCH, SEQ, D_MODEL),
            y[R_PROMPT:].reshape(DEC_BATCH, 1, D_MODEL),
            jnp.stack(states_p, axis=0),
            prompt_rows(k, D_MODEL).reshape(BATCH, t_all, B_HEADS, B_HD),
            prompt_rows(v, D_MODEL).reshape(BATCH, t_all, B_HEADS, B_HD),
            prompt_rows(lf, B_HEADS),
            jnp.stack(states_s, axis=0),
            k[R_PROMPT:].reshape(DEC_BATCH, 1, B_HEADS, B_HD),
            v[R_PROMPT:].reshape(DEC_BATCH, 1, B_HEADS, B_HD),
            lf[R_PROMPT:].reshape(DEC_BATCH, 1, B_HEADS))
```

```python
import functools

import jax
import jax.numpy as jnp
from jax import lax
from jax.experimental import pallas as pl
from jax.experimental.pallas import tpu as pltpu

F32 = jnp.float32
BF16 = jnp.bfloat16
HIGHEST = lax.Precision.HIGHEST

D_MODEL = 1024
BATCH = 8
SEQ = 2048
DEPTH = 4
DEC_BATCH = 128
PAST_LEN = 2048
PAGE_SIZE = 128
N_PAGES = PAST_LEN // PAGE_SIZE
N_META = 16
N_A_LAYERS = 2
A_HEADS = 4
A_DK_TOT = 512
A_DV_TOT = 1024
A_DK = 128
A_DV = 256
A_GATE_RANK = 16
A_GATE_TAU = 16.0
B_HEADS = 16
B_HD = 64
D_FF = 2816
EPS = 1e-6

R_MAIN = BATCH * SEQ
R_META = BATCH * N_META
R_SAMPLE = DEC_BATCH
R_PROMPT = R_MAIN + R_META
R_ALL = R_PROMPT + R_SAMPLE

V7X_LANES = 128
V7X_VMEM_LIMIT = 56 * 1024 * 1024

ROW_TILE = 640
TAIL_TILE = R_META + R_SAMPLE
FF_CHUNK = 1408
GLA_CHUNK = 128
GLA_SUB = 16
GLA_HEADS_PER_STEP = 2
FOX_TQ = 256
FOX_TK = 256
GLA_STEP_BB = 8


def _params(n_grid):
    return pltpu.CompilerParams(dimension_semantics=("arbitrary",) * n_grid,
                                vmem_limit_bytes=V7X_VMEM_LIMIT)


def _rmsnorm(x, g):
    return x * lax.rsqrt(jnp.mean(x * x, axis=-1, keepdims=True) + EPS) * g


def _log_sigmoid(z):
    return jnp.minimum(z, 0.0) - jnp.log1p(jnp.exp(-jnp.abs(z)))


def _silu(x):
    return x * jax.nn.sigmoid(x)


def _dot(a, b, precision=None):
    return jnp.dot(a, b, preferred_element_type=F32, precision=precision)


def _dot_nt(a, b):
    return lax.dot_general(a, b, (((1,), (1,)), ((), ())), preferred_element_type=F32)


def _dot_tn(a, b):
    return lax.dot_general(a, b, (((0,), (0,)), ((), ())), preferred_element_type=F32)


def _resident(a):
    nd = a.ndim
    return pl.BlockSpec(a.shape, lambda *_: (0,) * nd, pipeline_mode=pl.Buffered(1))


def _row_call(body, row_ins, res_ins, outs, name, tm=ROW_TILE):
    rows = row_ins[0].shape[0]
    in_specs = [pl.BlockSpec((tm, a.shape[1]), lambda i: (i, 0)) for a in row_ins]
    in_specs += [_resident(a) for a in res_ins]
    out_specs = [pl.BlockSpec((tm, c), lambda i: (i, 0)) for c, _ in outs]
    out_shape = [jax.ShapeDtypeStruct((rows, c), dt) for c, dt in outs]
    return pl.pallas_call(body, grid=(rows // tm,), in_specs=in_specs, out_specs=out_specs,
                          out_shape=out_shape, compiler_params=_params(1), name=name)(*row_ins, *res_ins)


def _split_row_call(body, main, tail, row_ins, res_ins, outs, name):
    tm = TAIL_TILE
    n_main = R_MAIN // tm
    in_specs = [pl.BlockSpec((tm, main.shape[1]), lambda i: (jnp.minimum(i, n_main - 1), 0)),
                pl.BlockSpec((tm, tail.shape[1]), lambda i: (0, 0))]
    in_specs += [pl.BlockSpec((tm, a.shape[1]), lambda i: (i, 0)) for a in row_ins]
    in_specs += [_resident(a) for a in res_ins]
    out_specs = [pl.BlockSpec((tm, c), lambda i: (i, 0)) for c, _ in outs]
    out_shape = [jax.ShapeDtypeStruct((R_ALL, c), dt) for c, dt in outs]
    return pl.pallas_call(body, grid=(R_ALL // tm,), in_specs=in_specs, out_specs=out_specs,
                          out_shape=out_shape, compiler_params=_params(1), name=name)(main, tail, *row_ins, *res_ins)


def _pick_tile(main_ref, tail_ref):
    is_tail = pl.program_id(0) == R_MAIN // TAIL_TILE
    return jnp.where(is_tail, tail_ref[...], main_ref[...])


def _ffn_body(x_ref, g_ref, wg_ref, wu_ref, wd_ref, o_ref):
    x = x_ref[...]
    n = _rmsnorm(x, g_ref[...]).astype(BF16)
    acc = jnp.zeros(x.shape, F32)
    for c in range(D_FF // FF_CHUNK):
        lo, hi = c * FF_CHUNK, (c + 1) * FF_CHUNK
        gate = _dot(n, wg_ref[:, lo:hi])
        up = _dot(n, wu_ref[:, lo:hi])
        acc = acc + _dot((_silu(gate) * up).astype(BF16), wd_ref[lo:hi, :])
    o_ref[...] = x + 0.5 * acc


def _ffn(x, g, wg, wu, wd):
    return _row_call(_ffn_body, [x], [g, wg, wu, wd], [(D_MODEL, F32)], "ffn")[0]


def _gla_proj_body(x_ref, g_ref, wq_ref, wk_ref, wv_ref, wr_ref, wa_ref, w2_ref, ba_ref,
                   q_ref, k_ref, v_ref, r_ref, la_ref):
    n = _rmsnorm(x_ref[...], g_ref[...]).astype(BF16)
    q_ref[...] = _dot(n, wq_ref[...]) * (A_DK ** -0.5)
    k_ref[...] = _dot(n, wk_ref[...])
    v_ref[...] = _dot(n, wv_ref[...])
    r_ref[...] = _dot(n, wr_ref[...])
    a = _dot(n, wa_ref[...]).astype(BF16)
    z = _dot(a, w2_ref[...]) + ba_ref[...]
    la_ref[...] = _log_sigmoid(z) * (1.0 / A_GATE_TAU)


def _gla_proj(x, g, w_in, w_alpha2, b_alpha):
    w = w_in.astype(BF16)
    c0, c1, c2, c3 = A_DK_TOT, 2 * A_DK_TOT, 2 * A_DK_TOT + A_DV_TOT, 2 * A_DK_TOT + 2 * A_DV_TOT
    wa = jnp.pad(w[:, c3:], ((0, 0), (0, V7X_LANES - A_GATE_RANK)))
    w2 = jnp.pad(w_alpha2.astype(BF16), ((0, V7X_LANES - A_GATE_RANK), (0, 0)))
    res = [g, w[:, :c0], w[:, c0:c1], w[:, c1:c2], w[:, c2:c3], wa, w2, b_alpha.reshape(1, A_DK_TOT)]
    outs = [(A_DK_TOT, F32), (A_DK_TOT, F32), (A_DV_TOT, F32), (A_DV_TOT, F32), (A_DK_TOT, F32)]
    return _row_call(_gla_proj_body, [x], res, outs, "gla_proj")


def _causal(c):
    return lax.broadcasted_iota(jnp.int32, (c, c), 0) >= lax.broadcasted_iota(jnp.int32, (c, c), 1)


def _cumulative_gate(la):
    return _dot(_causal(la.shape[0]).astype(F32), la, precision=HIGHEST)


def _gla_chunk(q, k, v, b, st, ecat_ref, n_sub):
    c = GLA_SUB * n_sub
    causal = _causal(c)
    a_rows = []
    for i in range(n_sub):
        lo = GLA_SUB * i
        b_i, q_i, k_i = b[lo:lo + GLA_SUB], q[lo:lo + GLA_SUB], k[lo:lo + GLA_SUB]
        pieces = []
        for s in range(GLA_SUB):
            decay = jnp.exp(jnp.minimum(b_i - b_i[s:s + 1], 0.0))
            pieces.append((q_i * k_i[s:s + 1] * decay).astype(BF16))
        a_i = _dot(jnp.concatenate(pieces, axis=1), ecat_ref[i])
        if i > 0:
            r_i = b[lo - 1:lo]
            q_s = (q_i * jnp.exp(b_i - r_i)).astype(BF16)
            k_s = (k[:lo] * jnp.exp(r_i - b[:lo])).astype(BF16)
            k_s = jnp.concatenate([k_s, jnp.zeros((c - lo, A_DK), BF16)], axis=0)
            a_i = a_i + _dot_nt(q_s, k_s)
        a_rows.append(a_i)
    a = jnp.concatenate(a_rows, axis=0)[:, :c]
    a = jnp.where(causal, a, 0.0)
    vb = v.astype(BF16)
    o = _dot(a.astype(BF16), vb)
    b_end = b[c - 1:c]
    k_d = (k * jnp.exp(b_end - b)).astype(BF16)
    st_new = _dot_tn(vb, k_d)
    if st is not None:
        o = o + _dot_nt((q * jnp.exp(b)).astype(BF16), st.astype(BF16))
        st_new = st_new + st * jnp.exp(b_end)
    return o, st_new


def _gla_scan_body(q_ref, k_ref, v_ref, la_ref, qt_ref, kt_ref, vt_ref, lat_ref, ecat_ref,
                   o_ref, ot_ref, s_ref, st_ref, b_ref):
    heads = [(slice(g * A_DK, (g + 1) * A_DK), slice(g * A_DV, (g + 1) * A_DV)) for g in range(GLA_HEADS_PER_STEP)]
    for g, (kl, vl) in enumerate(heads):
        o_meta, st = _gla_chunk(qt_ref[:, kl], kt_ref[:, kl], vt_ref[:, vl], _cumulative_gate(lat_ref[:, kl]),
                                None, ecat_ref, 1)
        ot_ref[:, vl] = o_meta
        st_ref[g] = st

    def gates(j, carry):
        rows = pl.ds(pl.multiple_of(j * GLA_CHUNK, GLA_CHUNK), GLA_CHUNK)
        for g, (kl, _) in enumerate(heads):
            b_ref[g, rows, :] = _cumulative_gate(la_ref[rows, kl])
        return carry

    lax.fori_loop(0, SEQ // GLA_CHUNK, gates, 0)

    def step(j, carry):
        rows = pl.ds(pl.multiple_of(j * GLA_CHUNK, GLA_CHUNK), GLA_CHUNK)
        for g, (kl, vl) in enumerate(heads):
            o, st_new = _gla_chunk(q_ref[rows, kl], k_ref[rows, kl], v_ref[rows, vl], b_ref[g, rows, :],
                                   st_ref[g], ecat_ref, GLA_CHUNK // GLA_SUB)
            o_ref[rows, vl] = o
            st_ref[g] = st_new
        return carry

    lax.fori_loop(0, SEQ // GLA_CHUNK, step, 0)
    for g in range(GLA_HEADS_PER_STEP):
        s_ref[0, g] = st_ref[g].T


def _gla_ecat():
    n_sub = GLA_CHUNK // GLA_SUB
    key = lax.broadcasted_iota(jnp.int32, (n_sub, GLA_SUB * A_DK, GLA_CHUNK), 1) // A_DK
    sub = lax.broadcasted_iota(jnp.int32, (n_sub, GLA_SUB * A_DK, GLA_CHUNK), 0)
    col = lax.broadcasted_iota(jnp.int32, (n_sub, GLA_SUB * A_DK, GLA_CHUNK), 2)
    return (col == GLA_SUB * sub + key).astype(BF16)


def _gla_scan(q, k, v, la):
    meta0 = R_MAIN // N_META
    n = GLA_HEADS_PER_STEP
    main_k = pl.BlockSpec((SEQ, n * A_DK), lambda b, h: (b, h))
    main_v = pl.BlockSpec((SEQ, n * A_DV), lambda b, h: (b, h))
    meta_k = pl.BlockSpec((N_META, n * A_DK), lambda b, h: (meta0 + b, h))
    meta_v = pl.BlockSpec((N_META, n * A_DV), lambda b, h: (meta0 + b, h))
    ecat = _gla_ecat()
    return pl.pallas_call(
        _gla_scan_body, grid=(BATCH, A_HEADS // n),
        in_specs=[main_k, main_k, main_v, main_k, meta_k, meta_k, meta_v, meta_k, _resident(ecat)],
        out_specs=[pl.BlockSpec((SEQ, n * A_DV), lambda b, h: (b, h)),
                   pl.BlockSpec((N_META, n * A_DV), lambda b, h: (b, h)),
                   pl.BlockSpec((1, n, A_DK, A_DV), lambda b, h: (b, h, 0, 0))],
        out_shape=[jax.ShapeDtypeStruct((R_MAIN, A_DV_TOT), F32),
                   jax.ShapeDtypeStruct((R_META, A_DV_TOT), F32),
                   jax.ShapeDtypeStruct((BATCH, A_HEADS, A_DK, A_DV), F32)],
        scratch_shapes=[pltpu.VMEM((n, A_DV, A_DK), F32), pltpu.VMEM((n, SEQ, A_DK), F32)],
        compiler_params=_params(2), name="gla_scan")(q, k, v, la, q, k, v, la, ecat)


def _gla_step_body(s0_ref, qt_ref, kt_ref, lat_ref, v_ref, s1_ref, o_ref):
    for bi in range(GLA_STEP_BB):
        for h in range(A_HEADS):
            cols = slice(h * A_DV, (h + 1) * A_DV)
            k_col = kt_ref[0, h, :, bi:bi + 1]
            q_col = qt_ref[0, h, :, bi:bi + 1]
            decay = jnp.exp(lat_ref[0, h, :, bi:bi + 1])
            s1 = decay * s0_ref[bi, h] + k_col * v_ref[bi:bi + 1, cols]
            s1_ref[bi, h] = s1
            o_ref[bi:bi + 1, cols] = jnp.sum(q_col * s1, axis=0, keepdims=True)


def _gla_step(s0, q, k, v, la):
    n_steps = DEC_BATCH // GLA_STEP_BB

    def cols(a):
        a = a[R_PROMPT:].reshape(n_steps, GLA_STEP_BB, A_HEADS, A_DK)
        return jnp.transpose(a, (0, 2, 3, 1))

    col_spec = pl.BlockSpec((1, A_HEADS, A_DK, GLA_STEP_BB), lambda i: (i, 0, 0, 0))
    st_spec = pl.BlockSpec((GLA_STEP_BB, A_HEADS, A_DK, A_DV), lambda i: (i, 0, 0, 0))
    v0 = R_PROMPT // GLA_STEP_BB
    return pl.pallas_call(
        _gla_step_body, grid=(n_steps,),
        in_specs=[st_spec, col_spec, col_spec, col_spec,
                  pl.BlockSpec((GLA_STEP_BB, A_DV_TOT), lambda i: (v0 + i, 0))],
        out_specs=[st_spec, pl.BlockSpec((GLA_STEP_BB, A_DV_TOT), lambda i: (i, 0))],
        out_shape=[jax.ShapeDtypeStruct(s0.shape, F32),
                   jax.ShapeDtypeStruct((R_SAMPLE, A_DV_TOT), F32)],
        compiler_params=_params(1), name="gla_step")(s0, cols(q), cols(k), cols(la), v)


def _gla_out_body(om_ref, ot_ref, r_ref, x_ref, gh_ref, w_ref, y_ref):
    o = _pick_tile(om_ref, ot_ref)
    heads = []
    for h in range(A_HEADS):
        heads.append(_rmsnorm(o[:, h * A_DV:(h + 1) * A_DV], gh_ref[...]))
    gated = jnp.concatenate(heads, axis=1) * _silu(r_ref[...])
    y_ref[...] = x_ref[...] + _dot(gated.astype(BF16), w_ref[...])


def _gla_out(o_main, o_tail, r, x, g_head, w_out):
    return _split_row_call(_gla_out_body, o_main, o_tail, [r, x], [g_head.reshape(1, A_DV), w_out.astype(BF16)],
                           [(D_MODEL, F32)], "gla_out")[0]


def _kv_proj_body(x_ref, g_ref, wk_ref, wv_ref, wf_ref, bf_ref, k_ref, v_ref, kb_ref, vb_ref, lf_ref):
    n = _rmsnorm(x_ref[...], g_ref[...]).astype(BF16)
    k = _dot(n, wk_ref[...])
    v = _dot(n, wv_ref[...])
    k_ref[...] = k
    v_ref[...] = v
    kb_ref[...] = k.astype(BF16)
    vb_ref[...] = v.astype(BF16)
    f = _dot(n, wf_ref[...])[:, :B_HEADS] + bf_ref[...]
    lf_ref[...] = _log_sigmoid(f)


def _kv_proj(x, g, w_in, b_f):
    w = w_in.astype(BF16)
    wf = jnp.pad(w[:, 2 * D_MODEL:], ((0, 0), (0, V7X_LANES - B_HEADS)))
    res = [g, w[:, :D_MODEL], w[:, D_MODEL:2 * D_MODEL], wf, b_f.reshape(1, B_HEADS)]
    outs = [(D_MODEL, F32), (D_MODEL, F32), (D_MODEL, BF16), (D_MODEL, BF16), (B_HEADS, F32)]
    return _row_call(_kv_proj_body, [x], res, outs, "kv_proj")


def _cumsum_body(lm_ref, lt_ref, cm_ref, ct_ref):
    def tri(n):
        return (lax.broadcasted_iota(jnp.int32, (n, n), 0) >=
                lax.broadcasted_iota(jnp.int32, (n, n), 1)).astype(F32)

    c_meta = _dot(tri(N_META), lt_ref[...], precision=HIGHEST)
    ct_ref[...] = c_meta
    tri_blk = tri(V7X_LANES)

    def step(j, carry):
        rows = pl.ds(pl.multiple_of(j * V7X_LANES, V7X_LANES), V7X_LANES)
        c = _dot(tri_blk, lm_ref[rows, :], precision=HIGHEST) + carry
        cm_ref[rows, :] = c
        return c[V7X_LANES - 1:V7X_LANES]

    lax.fori_loop(0, SEQ // V7X_LANES, step, c_meta[N_META - 1:N_META])


def _cumsum_prompt(lf):
    meta0 = R_MAIN // N_META
    return pl.pallas_call(
        _cumsum_body, grid=(BATCH,),
        in_specs=[pl.BlockSpec((SEQ, B_HEADS), lambda b: (b, 0)),
                  pl.BlockSpec((N_META, B_HEADS), lambda b: (meta0 + b, 0))],
        out_specs=[pl.BlockSpec((SEQ, B_HEADS), lambda b: (b, 0)),
                   pl.BlockSpec((N_META, B_HEADS), lambda b: (b, 0))],
        out_shape=[jax.ShapeDtypeStruct((R_MAIN, B_HEADS), F32),
                   jax.ShapeDtypeStruct((R_META, B_HEADS), F32)],
        compiler_params=_params(1), name="cumsum_logf")(lf, lf)


def _fox_q_body(x_ref, g_ref, w_ref, q_ref):
    n = _rmsnorm(x_ref[...], g_ref[...]).astype(BF16)
    q_ref[...] = (_dot(n, w_ref[...]) * (B_HD ** -0.5)).astype(BF16)


def _fox_q_sample(x, g, w_q):
    return _row_call(_fox_q_body, [x[R_PROMPT:]], [g, w_q.astype(BF16)], [(D_MODEL, BF16)], "fox_q_sample",
                     tm=R_SAMPLE)[0]


AUG_LANES = V7X_LANES
AUG_WIDTH = B_HEADS * AUG_LANES
AUG_K_BIAS = B_HD
AUG_Q_BIAS = B_HD + 3


def _split3(c):
    hi = c.astype(BF16)
    r1 = c - hi.astype(F32)
    mid = r1.astype(BF16)
    lo = (r1 - mid.astype(F32)).astype(BF16)
    return hi, mid, lo


def _aug_consts(bias_lane, ones_lane, sign):
    lane = lax.broadcasted_iota(jnp.int32, (3, B_HEADS, AUG_WIDTH), 2)
    head = lax.broadcasted_iota(jnp.int32, (3, B_HEADS, AUG_WIDTH), 1)
    part = lax.broadcasted_iota(jnp.int32, (3, B_HEADS, AUG_WIDTH), 0)
    place = jnp.where(lane == head * AUG_LANES + bias_lane + part, sign, 0.0).astype(BF16)
    in_group = lax.broadcasted_iota(jnp.int32, (1, AUG_WIDTH), 1) % AUG_LANES
    ones = jnp.where((in_group >= ones_lane) & (in_group < ones_lane + 3), 1.0, 0.0).astype(F32)
    return place, ones


def _add_bias_lanes(x, c, place_ref, ones_ref):
    hi, mid, lo = _split3(c)
    return x + _dot(hi, place_ref[0]) + _dot(mid, place_ref[1]) + _dot(lo, place_ref[2]) + ones_ref[...]


def _fox_qaug_body(x_ref, c_ref, g_ref, w_ref, place_ref, ones_ref, q_ref):
    n = _rmsnorm(x_ref[...], g_ref[...]).astype(BF16)
    q = _dot(n, w_ref[...]) * (B_HD ** -0.5)
    q_ref[...] = _add_bias_lanes(q, c_ref[...], place_ref, ones_ref).astype(BF16)


def _fox_qaug(x, c, g, w_q):
    w = jnp.pad(w_q.astype(BF16).reshape(D_MODEL, B_HEADS, B_HD), ((0, 0), (0, 0), (0, AUG_LANES - B_HD)))
    place, ones = _aug_consts(AUG_Q_BIAS, AUG_K_BIAS, 1.0)
    return _row_call(_fox_qaug_body, [x, c], [g, w.reshape(D_MODEL, AUG_WIDTH), place, ones],
                     [(AUG_WIDTH, BF16)], "fox_qaug")[0]


def _fox_kaug_body(k_ref, c_ref, spread_ref, place_ref, ones_ref, o_ref):
    k = _dot(k_ref[...], spread_ref[...])
    o_ref[...] = _add_bias_lanes(k, c_ref[...], place_ref, ones_ref).astype(BF16)


def _fox_kaug(kb, c):
    src = lax.broadcasted_iota(jnp.int32, (D_MODEL, AUG_WIDTH), 0)
    dst = lax.broadcasted_iota(jnp.int32, (D_MODEL, AUG_WIDTH), 1)
    spread = (dst == (src // B_HD) * AUG_LANES + src % B_HD).astype(BF16)
    place, ones = _aug_consts(AUG_K_BIAS, AUG_Q_BIAS, -1.0)
    return _row_call(_fox_kaug_body, [kb, c], [spread, place, ones], [(AUG_WIDTH, BF16)], "fox_kaug")[0]


def _fox_out_body(om_ref, ot_ref, x_ref, w_ref, y_ref):
    y_ref[...] = x_ref[...] + _dot(_pick_tile(om_ref, ot_ref), w_ref[...])


def _fox_out(o_main, o_tail, x, w_out):
    return _split_row_call(_fox_out_body, o_main, o_tail, [x], [w_out.astype(BF16)], [(D_MODEL, F32)],
                           "fox_out")[0]


def _fox_prompt_body(q_ref, qt_ref, k_ref, kt_ref, vT_ref, vTt_ref, vt_ref, o_ref, ot_ref,
                     m_ref, l_ref, a_ref, acc_ref, s_ref, p_ref):
    i = pl.program_id(1)
    visible = (lax.broadcasted_iota(jnp.int32, (FOX_TK, FOX_TQ), 0) <=
               lax.broadcasted_iota(jnp.int32, (FOX_TK, FOX_TQ), 1))

    m_ref[...] = jnp.full(m_ref.shape, -jnp.inf, F32)
    l_ref[...] = jnp.zeros(l_ref.shape, F32)
    acc_ref[...] = jnp.zeros(acc_ref.shape, F32)

    def attend(n_keys, keys_of, values_of, masked):
        for h in range(B_HEADS):
            grp = slice(h * AUG_LANES, (h + 1) * AUG_LANES)
            s_ref[h, :n_keys] = _dot_nt(keys_of(grp), q_ref[:, grp])
        for h in range(B_HEADS):
            for c in range(FOX_TQ // V7X_LANES):
                qs = slice(c * V7X_LANES, (c + 1) * V7X_LANES)
                s = s_ref[h, :n_keys, qs]
                if masked:
                    s = jnp.where(visible[:, qs], s, -jnp.inf)
                m_old = m_ref[h:h + 1, qs]
                m_new = jnp.maximum(m_old, jnp.max(s, axis=0, keepdims=True))
                alpha = jnp.exp(m_old - m_new)
                p = jnp.exp(s - m_new)
                p_ref[h, :n_keys, qs] = p.astype(BF16)
                m_ref[h:h + 1, qs] = m_new
                a_ref[h:h + 1, qs] = alpha
                l_ref[h:h + 1, qs] = alpha * l_ref[h:h + 1, qs] + jnp.sum(p, axis=0, keepdims=True)
        for h in range(B_HEADS):
            rows = slice(h * B_HD, (h + 1) * B_HD)
            acc_ref[rows, :] = a_ref[h:h + 1, :] * acc_ref[rows, :] + _dot(values_of(rows), p_ref[h, :n_keys])

    def key_block(j, masked):
        keys = pl.ds(pl.multiple_of(j * FOX_TK, FOX_TK), FOX_TK)
        attend(FOX_TK, lambda grp: k_ref[keys, grp], lambda rows: vT_ref[0, j, rows, :], masked)

    def full_block(j, carry):
        key_block(j, False)
        return carry

    attend(N_META, lambda grp: kt_ref[:, grp], lambda rows: vTt_ref[0, rows, :], False)
    lax.fori_loop(0, i, full_block, 0)
    key_block(i, True)

    for h in range(B_HEADS):
        rows = slice(h * B_HD, (h + 1) * B_HD)
        acc_ref[rows, :] = acc_ref[rows, :] / l_ref[h:h + 1, :]
    o_ref[...] = acc_ref[...].T.astype(o_ref.dtype)

    @pl.when(i == 0)
    def _():
        tri = (lax.broadcasted_iota(jnp.int32, (N_META, N_META), 0) >=
               lax.broadcasted_iota(jnp.int32, (N_META, N_META), 1))
        for h in range(B_HEADS):
            grp = slice(h * AUG_LANES, (h + 1) * AUG_LANES)
            hs = slice(h * B_HD, (h + 1) * B_HD)
            s = jnp.where(tri, _dot_nt(qt_ref[:, grp], kt_ref[:, grp]), -jnp.inf)
            p = jnp.exp(s - jnp.max(s, axis=1, keepdims=True))
            o = _dot(p.astype(BF16), vt_ref[:, hs]) / jnp.sum(p, axis=1, keepdims=True)
            ot_ref[:, hs] = o.astype(ot_ref.dtype)


def _fox_values_t(vb):
    vT = jnp.transpose(vb[:R_MAIN].reshape(BATCH, SEQ // FOX_TK, FOX_TK, D_MODEL), (0, 1, 3, 2))
    vTt = jnp.transpose(vb[R_MAIN:R_PROMPT].reshape(BATCH, N_META, D_MODEL), (0, 2, 1))
    return vT, vTt


def _fox_prompt(qaug, kaug, vb, vT, vTt):
    meta0 = R_MAIN // N_META
    n_qblk = SEQ // FOX_TQ
    n_kblk = SEQ // FOX_TK
    return pl.pallas_call(
        _fox_prompt_body, grid=(BATCH, n_qblk),
        in_specs=[pl.BlockSpec((FOX_TQ, AUG_WIDTH), lambda b, i: (b * n_qblk + i, 0)),
                  pl.BlockSpec((N_META, AUG_WIDTH), lambda b, i: (meta0 + b, 0)),
                  pl.BlockSpec((SEQ, AUG_WIDTH), lambda b, i: (b, 0)),
                  pl.BlockSpec((N_META, AUG_WIDTH), lambda b, i: (meta0 + b, 0)),
                  pl.BlockSpec((1, n_kblk, D_MODEL, FOX_TK), lambda b, i: (b, 0, 0, 0)),
                  pl.BlockSpec((1, D_MODEL, N_META), lambda b, i: (b, 0, 0)),
                  pl.BlockSpec((N_META, D_MODEL), lambda b, i: (meta0 + b, 0))],
        out_specs=[pl.BlockSpec((FOX_TQ, D_MODEL), lambda b, i: (b * n_qblk + i, 0)),
                   pl.BlockSpec((N_META, D_MODEL), lambda b, i: (b, 0))],
        out_shape=[jax.ShapeDtypeStruct((R_MAIN, D_MODEL), BF16),
                   jax.ShapeDtypeStruct((R_META, D_MODEL), BF16)],
        scratch_shapes=[pltpu.VMEM((B_HEADS, FOX_TQ), F32), pltpu.VMEM((B_HEADS, FOX_TQ), F32),
                        pltpu.VMEM((B_HEADS, FOX_TQ), F32), pltpu.VMEM((D_MODEL, FOX_TQ), F32),
                        pltpu.VMEM((B_HEADS, FOX_TK, FOX_TQ), F32),
                        pltpu.VMEM((B_HEADS, FOX_TK, FOX_TQ), BF16)],
        compiler_params=_params(2), name="fox_prompt")(qaug, qaug, kaug, kaug, vT, vTt, vb)


def _fox_decode_body(pt_ref, q_ref, qc_ref, kn_ref, vn_ref, lfn_ref, *refs):
    n = N_PAGES
    k_refs, v_refs, lf_refs = refs[:n], refs[n:2 * n], refs[2 * n:3 * n]
    o_ref, s_ref, acc_ref = refs[3 * n:]

    later = (lax.broadcasted_iota(jnp.int32, (PAGE_SIZE, PAGE_SIZE), 0) >
             lax.broadcasted_iota(jnp.int32, (PAGE_SIZE, PAGE_SIZE), 1)).astype(F32)
    suf = lfn_ref[0]
    for j in reversed(range(n)):
        lf = lf_refs[j][0]
        bias = _dot(lf, later, precision=HIGHEST) + suf
        suf = bias[:, 0:1] + lf[:, 0:1]
        s_ref[:, j * PAGE_SIZE:(j + 1) * PAGE_SIZE] = bias

    for h in range(B_HEADS):
        q_col = jnp.broadcast_to(qc_ref[0, h], (B_HD, PAGE_SIZE))
        for j in range(n):
            lanes = slice(j * PAGE_SIZE, (j + 1) * PAGE_SIZE)
            qk = jnp.sum(q_col * k_refs[j][0, h], axis=0, keepdims=True)
            s_ref[h:h + 1, lanes] = s_ref[h:h + 1, lanes] + qk

    lane_head = lax.broadcasted_iota(jnp.int32, (B_HEADS, D_MODEL), 1) // B_HD
    head = lax.broadcasted_iota(jnp.int32, (B_HEADS, D_MODEL), 0)
    own = jnp.where(lane_head == head, 1.0, 0.0)
    k_new = kn_ref[0].astype(BF16).astype(F32)
    s_new = jnp.sum(q_ref[0].astype(F32) * own * k_new, axis=1, keepdims=True)
    s = s_ref[...]
    m = jnp.maximum(jnp.max(s, axis=1, keepdims=True), s_new)
    p = jnp.exp(s - m)
    p_new = jnp.exp(s_new - m)
    l = jnp.sum(p, axis=1, keepdims=True) + p_new
    s_ref[...] = p

    for h in range(B_HEADS):
        acc = jnp.zeros((B_HD, PAGE_SIZE), F32)
        for j in range(n):
            acc = acc + s_ref[h:h + 1, j * PAGE_SIZE:(j + 1) * PAGE_SIZE] * v_refs[j][0, h]
        acc_ref[h * B_HD:(h + 1) * B_HD, :] = acc

    lane_sum = lax.dot_general(jnp.ones((8, PAGE_SIZE), F32), acc_ref[...], (((1,), (1,)), ((), ())),
                               preferred_element_type=F32, precision=HIGHEST)[0:1]
    new = jnp.sum(p_new * own, axis=0, keepdims=True) * vn_ref[0]
    o_ref[0] = ((lane_sum + new) / jnp.sum(l * own, axis=0, keepdims=True)).astype(o_ref.dtype)


def _cache_views(cache_k, cache_v, cache_logf):
    return (jnp.transpose(cache_k, (0, 2, 3, 1)), jnp.transpose(cache_v, (0, 2, 3, 1)),
            jnp.transpose(cache_logf, (0, 2, 1)))


def _fox_decode(page_table, q_s, k, v, lf, ck_t, cv_t, clf_t):
    n = N_PAGES

    def new_rows(a):
        return a[R_PROMPT:].reshape(DEC_BATCH, 1, D_MODEL)

    def page_spec(j, tail):
        return pl.BlockSpec((1, B_HEADS) + tail,
                            lambda b, pt: (pt[b, j],) + (0,) * (1 + len(tail)))

    row = pl.BlockSpec((1, 1, D_MODEL), lambda b, pt: (b, 0, 0))
    in_specs = [row, pl.BlockSpec((1, B_HEADS, B_HD, 1), lambda b, pt: (b, 0, 0, 0)), row, row,
                pl.BlockSpec((1, B_HEADS, 1), lambda b, pt: (b, 0, 0))]
    in_specs += [page_spec(j, (B_HD, PAGE_SIZE)) for j in range(n)]
    in_specs += [page_spec(j, (B_HD, PAGE_SIZE)) for j in range(n)]
    in_specs += [page_spec(j, (PAGE_SIZE,)) for j in range(n)]
    grid_spec = pltpu.PrefetchScalarGridSpec(
        num_scalar_prefetch=1, grid=(DEC_BATCH,), in_specs=in_specs, out_specs=row,
        scratch_shapes=[pltpu.VMEM((B_HEADS, PAST_LEN), F32), pltpu.VMEM((D_MODEL, PAGE_SIZE), F32)])
    q_cols = q_s.astype(F32).reshape(DEC_BATCH, B_HEADS, B_HD, 1)
    lf_new = lf[R_PROMPT:].reshape(DEC_BATCH, B_HEADS, 1)
    o = pl.pallas_call(
        _fox_decode_body, grid_spec=grid_spec,
        out_shape=jax.ShapeDtypeStruct((DEC_BATCH, 1, D_MODEL), BF16),
        compiler_params=_params(1), name="fox_decode")(
            page_table, q_s.reshape(DEC_BATCH, 1, D_MODEL), q_cols, new_rows(k), new_rows(v), lf_new,
            *([ck_t] * n), *([cv_t] * n), *([clf_t] * n))
    return o.reshape(DEC_BATCH, D_MODEL)


def _final_norm_body(x_ref, g_ref, y_ref):
    y_ref[...] = _rmsnorm(x_ref[...], g_ref[...])


def _final_norm(x, g):
    def call(n_tiles, first_tile, name):
        return pl.pallas_call(
            _final_norm_body, grid=(n_tiles,),
            in_specs=[pl.BlockSpec((TAIL_TILE, D_MODEL), lambda i: (first_tile + i, 0)), _resident(g)],
            out_specs=pl.BlockSpec((TAIL_TILE, D_MODEL), lambda i: (i, 0)),
            out_shape=jax.ShapeDtypeStruct((n_tiles * TAIL_TILE, D_MODEL), F32),
            compiler_params=_params(1), name=name)(x, g)

    n_main = R_MAIN // TAIL_TILE
    return call(n_main, 0, "final_norm"), call(1, n_main, "final_norm_tail")


def kernel(x_prompt, x_sample, state_gla, cache_k, cache_v, cache_logf, page_table, meta_tokens,
           norm_ffn1, ffn1_w_gate, ffn1_w_up, ffn1_w_down, norm_mix, norm_ffn2,
           ffn2_w_gate, ffn2_w_up, ffn2_w_down, gla_w_in, gla_w_alpha2, gla_b_alpha, gla_g_head,
           gla_w_out, kv_norm, kv_w_in, kv_b_f, fox_w_q, fox_w_out, final_norm):
    def gain(g):
        return g.reshape(1, D_MODEL)

    meta = jnp.broadcast_to(meta_tokens[None], (BATCH, N_META, D_MODEL)).reshape(R_META, D_MODEL)
    x = jnp.concatenate([x_prompt.reshape(R_MAIN, D_MODEL), meta, x_sample.reshape(R_SAMPLE, D_MODEL)], axis=0)

    states_p, states_s = [], []
    k = v = vb = vT = vTt = c_rows = kaug = None
    ck_t, cv_t, clf_t = _cache_views(cache_k, cache_v, cache_logf)
    for l in range(DEPTH):
        x = _ffn(x, gain(norm_ffn1[l]), ffn1_w_gate[l].astype(BF16), ffn1_w_up[l].astype(BF16),
                 ffn1_w_down[l].astype(BF16))
        if l < N_A_LAYERS:
            q, kk, vv, r, la = _gla_proj(x, gain(norm_mix[l]), gla_w_in[l], gla_w_alpha2[l], gla_b_alpha[l])
            o_main, o_meta, st_p = _gla_scan(q, kk, vv, la)
            st_s, o_s = _gla_step(state_gla[l], q, kk, vv, la)
            states_p.append(st_p)
            states_s.append(st_s)
            x = _gla_out(o_main, jnp.concatenate([o_meta, o_s], axis=0), r, x, gla_g_head[l], gla_w_out[l])
        else:
            lb = l - N_A_LAYERS
            qaug = _fox_qaug(x, c_rows, gain(norm_mix[l]), fox_w_q[lb])
            o_main, o_meta = _fox_prompt(qaug, kaug, vb, vT, vTt)
            q_s = _fox_q_sample(x, gain(norm_mix[l]), fox_w_q[lb])
            o_s = _fox_decode(page_table, q_s, k, v, lf, ck_t, cv_t, clf_t)
            x = _fox_out(o_main, jnp.concatenate([o_meta, o_s], axis=0), x, fox_w_out[lb])
        x = _ffn(x, gain(norm_ffn2[l]), ffn2_w_gate[l].astype(BF16), ffn2_w_up[l].astype(BF16),
                 ffn2_w_down[l].astype(BF16))
        if l == N_A_LAYERS - 1:
            k, v, kb, vb, lf = _kv_proj(x, gain(kv_norm), kv_w_in, kv_b_f)
            c_main, c_meta = _cumsum_prompt(lf)
            c_rows = jnp.concatenate([c_main, c_meta, jnp.zeros((R_SAMPLE, B_HEADS), F32)], axis=0)
            kaug = _fox_kaug(kb, c_rows)
            vT, vTt = _fox_values_t(vb)
    y_main, y_tail = _final_norm(x, gain(final_norm))

    def prompt_rows(a, width):
        return jnp.concatenate([a[R_MAIN:R_PROMPT].reshape(BATCH, N_META, width),
                                a[:R_MAIN].reshape(BATCH, SEQ, width)], axis=1)

    t_all = N_META + SEQ
    return (y_main.reshape(BATCH, SEQ, D_MODEL),
            y_tail[R_META:].reshape(DEC_BATCH, 1, D_MODEL),
            jnp.stack(states_p, axis=0),
            prompt_rows(k, D_MODEL).reshape(BATCH, t_all, B_HEADS, B_HD),
            prompt_rows(v, D_MODEL).reshape(BATCH, t_all, B_HEADS, B_HD),
            prompt_rows(lf, B_HEADS),
            jnp.stack(states_s, axis=0),
            k[R_PROMPT:].reshape(DEC_BATCH, 1, B_HEADS, B_HD),
            v[R_PROMPT:].reshape(DEC_BATCH, 1, B_HEADS, B_HD),
            lf[R_PROMPT:].reshape(DEC_BATCH, 1, B_HEADS))
```
